```python
import jax, jax.numpy as jnp
from jax import lax
import numpy as np

D_MODEL = 2048
BATCH = 8
SEQ = 2048
DEPTH = 2
DEC_BATCH = 128
DEC_SEQ = 4
PAST_LEN = 2048
PAGE_SIZE = 128

A_DK = 128
A_DV = 128
A_HEADS = D_MODEL // 256
A_KW = A_HEADS * A_DK
A_W = A_HEADS * A_DV
A_CHUNK = 32
B_W = D_MODEL // 2
B_BLOCKS = 8
B_BS = B_W // B_BLOCKS
CONV_W = 4
LRU_C = 8.0
C_DH = 128
C_HEADS = D_MODEL // 256
C_W = C_HEADS * C_DH
Q_BLOCK = 128
SB_BIAS_INIT = -7.0
EPS = 1e-6

kernel_name = "hybrid_hgrn2_rglru_stickbreaking_step"


def _split_sizes():
    return (A_KW, A_KW, A_W, A_W,
            B_W, B_W,
            C_W, C_W, C_W, C_W,
            D_MODEL, D_MODEL, D_MODEL)


def _split_points():
    pts, acc = [], 0
    for s in _split_sizes()[:-1]:
        acc += s
        pts.append(acc)
    return pts


def rmsnorm(x, g):
    xf = x.astype(jnp.float32)
    return xf * lax.rsqrt(jnp.mean(xf * xf, axis=-1, keepdims=True) + EPS) * g.astype(jnp.float32)


def hgrn2_chunked(q, k, log_f, v, s0):
    bsz, seqlen, h, dk = q.shape
    dv = v.shape[-1]
    c = min(A_CHUNK, seqlen)
    n = -(-seqlen // c)
    pad = n * c - seqlen
    padw = ((0, 0), (0, pad), (0, 0), (0, 0))
    q, k, log_f, v = (jnp.pad(t, padw) for t in (q, k, log_f, v))
    to_chunks = lambda t: t.reshape(bsz, n, c, h, t.shape[-1]).transpose(1, 0, 3, 2, 4)
    xs = (to_chunks(q), to_chunks(k), to_chunks(log_f), to_chunks(v))
    causal = jnp.tril(jnp.ones((c, c), dtype=bool))[:, :, None]

    def step(s, chunk):
        qc, kc, fc, vc = chunk
        cum = jnp.cumsum(fc, axis=2)
        o_inter = jnp.einsum('bhtk,bhkv->bhtv', qc * jnp.exp(cum), s)
        diff = cum[:, :, :, None, :] - cum[:, :, None, :, :]
        decay = jnp.exp(jnp.where(causal, diff, -jnp.inf))
        att = jnp.einsum('bhtk,bhsk,bhtsk->bhts', qc, kc, decay)
        o = o_inter + jnp.einsum('bhts,bhsv->bhtv', att, vc)
        last = cum[:, :, -1]
        s_new = jnp.exp(last)[..., None] * s + jnp.einsum(
            'bhsk,bhsv->bhkv', kc * jnp.exp(last[:, :, None, :] - cum), vc)
        return s_new, o

    s_fin, o = lax.scan(step, s0, xs)
    o = o.transpose(1, 0, 3, 2, 4).reshape(bsz, n * c, h, dv)[:, :seqlen]
    return o, s_fin


def rg_lru_branch(xb, buf0, h0, conv_w, conv_b, w_r, b_r, w_i, b_i, lam):
    f32 = jnp.float32
    bsz, seqlen, width = xb.shape
    xp = jnp.concatenate([buf0.astype(f32), xb], axis=1)
    xc = conv_b.astype(f32) + sum(xp[:, j:j + seqlen] * conv_w[j].astype(f32) for j in range(CONV_W))
    new_buf = xp[:, seqlen:]
    xblk = xc.reshape(bsz, seqlen, B_BLOCKS, B_BS)
    r = jax.nn.sigmoid(jnp.einsum('blnc,ncd->blnd', xblk, w_r.astype(f32)).reshape(bsz, seqlen, width) + b_r)
    i = jax.nn.sigmoid(jnp.einsum('blnc,ncd->blnd', xblk, w_i.astype(f32)).reshape(bsz, seqlen, width) + b_i)
    log_a = -LRU_C * r * jax.nn.softplus(-lam.astype(f32))
    a = jnp.exp(log_a)
    u = jnp.sqrt(-jnp.expm1(2.0 * log_a)) * (i * xc)

    def step(h, au):
        a_t, u_t = au
        h = a_t * h + u_t
        return h, h

    h_fin, hs = lax.scan(step, h0.astype(f32), (a.swapaxes(0, 1), u.swapaxes(0, 1)))
    return hs.swapaxes(0, 1), h_fin, new_buf


def stick_breaking(q, k, v, sb_bias, q_pos, k_pos):
    bsz, lq, h, dh = q.shape
    qb = min(Q_BLOCK, lq)
    nb = -(-lq // qb)
    pad = nb * qb - lq
    q = jnp.pad(q, ((0, 0), (0, pad), (0, 0), (0, 0)))
    q_pos = jnp.pad(q_pos, (0, pad), constant_values=-1)
    qs = q.reshape(bsz, nb, qb, h, dh).transpose(1, 0, 2, 3, 4)
    ps = q_pos.reshape(nb, qb)
    scale = dh ** -0.5
    bias = sb_bias.astype(jnp.float32)[None, :, None, None]

    def one_block(args):
        qblk, pblk = args
        z = jnp.einsum('bqhd,bkhd->bhqk', qblk, k) * scale + bias
        mask = k_pos[None, :] < pblk[:, None]
        log_keep = jnp.where(mask, jax.nn.log_sigmoid(-z), 0.0)
        after = lax.cumsum(log_keep, axis=3, reverse=True) - log_keep
        w = jnp.where(mask, jnp.exp(jax.nn.log_sigmoid(z) + after), 0.0)
        return jnp.einsum('bhqk,bkhd->bqhd', w, v)

    o = lax.map(one_block, (qs, ps))
    return o.transpose(1, 0, 2, 3, 4).reshape(bsz, nb * qb, h, dh)[:, :lq]


def mixer_layer(x, pos0, k_past, v_past, s0, h0, buf0, g_pre, g_post, w_in, lb, g_hgrn, conv_w, conv_b,
                w_r, b_r, w_i, b_i, lam, sb_bias, w_pa, w_pb, w_pc, w_out):
    f32 = jnp.float32
    bsz, seqlen, _ = x.shape
    xn = rmsnorm(x, g_pre)
    proj = jnp.matmul(xn, w_in.astype(f32))
    (a_q, a_f, a_i, a_g, b_x, b_g, c_q, c_k, c_v, c_g, m_a, m_b, m_c) = jnp.split(proj, _split_points(), axis=-1)

    head_k = (bsz, seqlen, A_HEADS, A_DK)
    lbh = lb.reshape(A_HEADS, A_DK)
    zf = a_f.reshape(head_k)
    log_f = jnp.logaddexp(jnp.log(lbh), jnp.log1p(-lbh) + jax.nn.log_sigmoid(zf))
    k_in = (1.0 - lbh) * jax.nn.sigmoid(-zf)
    q_a = jax.nn.silu(a_q).reshape(head_k)
    v_a = a_i.reshape(bsz, seqlen, A_HEADS, A_DV)
    o_a, s_new = hgrn2_chunked(q_a, k_in, log_f, v_a, s0.astype(f32))
    o_a = rmsnorm(o_a, g_hgrn.reshape(A_HEADS, A_DV)).reshape(bsz, seqlen, A_W) * jax.nn.silu(a_g)

    o_b, h_new, buf_new = rg_lru_branch(b_x, buf0, h0, conv_w, conv_b, w_r, b_r, w_i, b_i, lam)
    o_b = o_b * jax.nn.silu(b_g)

    head_c = (bsz, seqlen, C_HEADS, C_DH)
    q_c, k_c, v_c = c_q.reshape(head_c), c_k.reshape(head_c), c_v.reshape(head_c)
    if k_past is None:
        k_all, v_all = k_c, v_c
    else:
        k_all = jnp.concatenate([k_past.astype(f32), k_c], axis=1)
        v_all = jnp.concatenate([v_past.astype(f32), v_c], axis=1)
    q_pos = pos0 + jnp.arange(seqlen, dtype=jnp.int32)
    k_pos = jnp.arange(k_all.shape[1], dtype=jnp.int32)
    o_c = stick_breaking(q_c, k_all, v_all, sb_bias, q_pos, k_pos).reshape(bsz, seqlen, C_W) * jax.nn.silu(c_g)

    merged = (jax.nn.sigmoid(m_a) * jnp.matmul(o_a, w_pa.astype(f32))
              + jax.nn.sigmoid(m_b) * jnp.matmul(o_b, w_pb.astype(f32))
              + jax.nn.sigmoid(m_c) * jnp.matmul(o_c, w_pc.astype(f32)))
    y = x.astype(f32) + rmsnorm(jnp.matmul(merged, w_out.astype(f32)), g_post)
    return y.astype(x.dtype), k_c.astype(x.dtype), v_c.astype(x.dtype), s_new, h_new, buf_new


def setup_inputs(seed: int = 0) -> dict:
    key = jax.random.key(seed)
    ks = jax.random.split(key, 25)
    f32 = jnp.float32
    n_pages = PAST_LEN // PAGE_SIZE
    n_phys = (DEC_BATCH * n_pages * 5) // 4
    in_cols = sum(_split_sizes())

    def nrm(k, shape, s):
        return jax.random.normal(k, shape, f32) * s

    x_prompt = nrm(ks[0], (BATCH, SEQ, D_MODEL), 1.0)
    x_sample = nrm(ks[1], (DEC_BATCH, DEC_SEQ, D_MODEL), 1.0)
    cache_k = nrm(ks[2], (DEPTH, n_phys, PAGE_SIZE, C_HEADS, C_DH), 1.0)
    cache_v = nrm(ks[3], (DEPTH, n_phys, PAGE_SIZE, C_HEADS, C_DH), 1.0)
    state_hgrn = nrm(ks[4], (DEPTH, DEC_BATCH, A_HEADS, A_DK, A_DV), 0.5)
    state_lru_h = nrm(ks[5], (DEPTH, DEC_BATCH, B_W), 0.5)
    state_conv = nrm(ks[6], (DEPTH, DEC_BATCH, CONV_W - 1, B_W), 1.0)
    page_table = jax.random.permutation(ks[7], n_phys)[: DEC_BATCH * n_pages].reshape(
        DEC_BATCH, n_pages).astype(jnp.int32)
    norm_pre = 1.0 + nrm(ks[8], (DEPTH, D_MODEL), 0.05)
    norm_post = 1.0 + nrm(ks[9], (DEPTH, D_MODEL), 0.05)
    w_in = nrm(ks[10], (DEPTH, D_MODEL, in_cols), D_MODEL ** -0.5)
    hgrn_lb_logits = nrm(ks[11], (DEPTH, A_KW), 1.0)
    hgrn_norm = 1.0 + nrm(ks[12], (DEPTH, A_W), 0.05)
    lru_conv_w = nrm(ks[13], (DEPTH, CONV_W, B_W), CONV_W ** -0.5)
    lru_conv_b = nrm(ks[14], (DEPTH, B_W), 0.01)
    lru_w_r = nrm(ks[15], (DEPTH, B_BLOCKS, B_BS, B_BS), B_BS ** -0.5)
    lru_b_r = nrm(ks[16], (DEPTH, B_W), 0.01)
    lru_w_i = nrm(ks[17], (DEPTH, B_BLOCKS, B_BS, B_BS), B_BS ** -0.5)
    lru_b_i = nrm(ks[18], (DEPTH, B_W), 0.01)
    u = jax.random.uniform(ks[19], (DEPTH, B_W), f32, 0.9, 0.999)
    a0 = u ** (1.0 / LRU_C)
    lru_lambda = jnp.log(a0) - jnp.log1p(-a0)
    sb_bias = SB_BIAS_INIT + nrm(ks[24], (DEPTH, C_HEADS), 0.1)
    w_branch_a = nrm(ks[20], (DEPTH, A_W, D_MODEL), A_W ** -0.5)
    w_branch_b = nrm(ks[21], (DEPTH, B_W, D_MODEL), B_W ** -0.5)
    w_branch_c = nrm(ks[22], (DEPTH, C_W, D_MODEL), C_W ** -0.5)
    w_out = nrm(ks[23], (DEPTH, D_MODEL, D_MODEL), D_MODEL ** -0.5)
    return {"x_prompt": x_prompt, "x_sample": x_sample, "cache_k": cache_k, "cache_v": cache_v,
            "state_hgrn": state_hgrn, "state_lru_h": state_lru_h, "state_conv": state_conv,
            "page_table": page_table, "norm_pre": norm_pre, "norm_post": norm_post, "w_in": w_in,
            "hgrn_lb_logits": hgrn_lb_logits, "hgrn_norm": hgrn_norm, "lru_conv_w": lru_conv_w,
            "lru_conv_b": lru_conv_b, "lru_w_r": lru_w_r, "lru_b_r": lru_b_r, "lru_w_i": lru_w_i,
            "lru_b_i": lru_b_i, "lru_lambda": lru_lambda, "sb_bias": sb_bias, "w_branch_a": w_branch_a,
            "w_branch_b": w_branch_b, "w_branch_c": w_branch_c, "w_out": w_out}


def reference(x_prompt, x_sample, cache_k, cache_v, state_hgrn, state_lru_h, state_conv, page_table,
              norm_pre, norm_post, w_in, hgrn_lb_logits, hgrn_norm, lru_conv_w, lru_conv_b, lru_w_r, lru_b_r,
              lru_w_i, lru_b_i, lru_lambda, sb_bias, w_branch_a, w_branch_b, w_branch_c, w_out):
    f32 = jnp.float32
    lbs = jnp.cumsum(jax.nn.softmax(hgrn_lb_logits.astype(f32), axis=0), axis=0)
    lbs = lbs - lbs[0:1]
    n_pages = page_table.shape[1]
    past_len = n_pages * cache_k.shape[2]
    pb = x_prompt.shape[0]
    sb = x_sample.shape[0]
    hp, hs = x_prompt, x_sample
    pk, pv, ps, ph, pc = [], [], [], [], []
    sk, sv, ss, sh, sc = [], [], [], [], []
    for l in range(DEPTH):
        params = (norm_pre[l], norm_post[l], w_in[l], lbs[l], hgrn_norm[l], lru_conv_w[l], lru_conv_b[l],
                  lru_w_r[l], lru_b_r[l], lru_w_i[l], lru_b_i[l], lru_lambda[l], sb_bias[l],
                  w_branch_a[l], w_branch_b[l], w_branch_c[l], w_out[l])
        s0 = jnp.zeros((pb, A_HEADS, A_DK, A_DV), f32)
        h0 = jnp.zeros((pb, B_W), f32)
        b0 = jnp.zeros((pb, CONV_W - 1, B_W), f32)
        hp, k_new, v_new, s_new, h_new, buf_new = mixer_layer(hp, 0, None, None, s0, h0, b0, *params)
        pk.append(k_new); pv.append(v_new); ps.append(s_new); ph.append(h_new); pc.append(buf_new)
        k_past = cache_k[l][page_table].reshape(sb, past_len, C_HEADS, C_DH)
        v_past = cache_v[l][page_table].reshape(sb, past_len, C_HEADS, C_DH)
        hs, k_new, v_new, s_new, h_new, buf_new = mixer_layer(
            hs, past_len, k_past, v_past, state_hgrn[l], state_lru_h[l], state_conv[l], *params)
        sk.append(k_new); sv.append(v_new); ss.append(s_new); sh.append(h_new); sc.append(buf_new)
    return (hp, hs, jnp.stack(pk), jnp.stack(pv), jnp.stack(ps), jnp.stack(ph), jnp.stack(pc),
            jnp.stack(sk), jnp.stack(sv), jnp.stack(ss), jnp.stack(sh), jnp.stack(sc))
```

```python
import functools
import math

import numpy as np
import jax
import jax.numpy as jnp
from jax import lax
from jax.experimental import pallas as pl
from jax.experimental.pallas import tpu as pltpu

F32 = jnp.float32
BF16 = jnp.bfloat16
EPS = 1e-6
LRU_C = 8.0
LANES = 128
SUBLANES = 8
VMEM_LIMIT = 56 * 1024 * 1024
HIGHEST = lax.Precision.HIGHEST

_NT = (((1,), (1,)), ((), ()))
_TN = (((0,), (0,)), ((), ()))


def _cparams(sem):
    return pltpu.CompilerParams(dimension_semantics=sem, vmem_limit_bytes=VMEM_LIMIT)


def _sigmoid(x):
    e = jnp.exp(-jnp.abs(x))
    r = 1.0 / (1.0 + e)
    return jnp.where(x >= 0, r, e * r)


def _softplus(x):
    return jnp.maximum(x, 0.0) + jnp.log1p(jnp.exp(-jnp.abs(x)))


def _inproj_kernel(x_ref, g_ref, w_ref, main_ref, gate_ref, xn_ref, *, n_main_blocks):
    j = pl.program_id(1)

    @pl.when(j == 0)
    def _():
        x = x_ref[...]
        ms = jnp.mean(x * x, axis=-1, keepdims=True)
        xn_ref[...] = (x * lax.rsqrt(ms + EPS) * g_ref[...]).astype(BF16)

    acc = jnp.dot(xn_ref[...], w_ref[...], preferred_element_type=F32)

    @pl.when(j < n_main_blocks)
    def _():
        main_ref[...] = acc

    @pl.when(j >= n_main_blocks)
    def _():
        gate_ref[...] = _sigmoid(acc).astype(BF16)


def _inproj(x2d, g_all, w_all, layer, n_main, bm, bn):
    m, d = x2d.shape
    n = w_all.shape[2]
    nmb = n_main // bn
    nb = n // bn
    kern = functools.partial(_inproj_kernel, n_main_blocks=nmb)
    return pl.pallas_call(
        kern,
        grid=(m // bm, nb),
        in_specs=[
            pl.BlockSpec((bm, d), lambda i, j: (i, 0)),
            pl.BlockSpec((None, 1, d), lambda i, j: (layer, 0, 0)),
            pl.BlockSpec((None, d, bn), lambda i, j: (layer, 0, j)),
        ],
        out_specs=[
            pl.BlockSpec((bm, bn), lambda i, j: (i, jnp.minimum(j, nmb - 1))),
            pl.BlockSpec((bm, bn), lambda i, j: (i, jnp.maximum(j - nmb, 0))),
        ],
        out_shape=[
            jax.ShapeDtypeStruct((m, n_main), F32),
            jax.ShapeDtypeStruct((m, n - n_main), BF16),
        ],
        scratch_shapes=[pltpu.VMEM((bm, d), BF16)],
        compiler_params=_cparams(("parallel", "arbitrary")),
        name="inproj",
    )(x2d, g_all, w_all)


def _hgrn_level_matrix(c):
    nlev = int(math.log2(c))
    assert 1 << nlev == c
    t = np.arange(c)[:, None]
    j = np.arange(c)[None, :]
    mats = []
    for lev in range(nlev):
        m = 2 << lev
        mid = (t // m) * m + m // 2 - 1
        second = (t % m) >= m // 2
        q_side = second & (j > mid) & (j <= t)
        k_side = (~second) & (j > t) & (j <= mid)
        mats.append(q_side | k_side)
    mats.append(j <= t)
    mats.append(j > t)
    return np.concatenate(mats, axis=0).astype(np.float32), nlev


def _hgrn_kernel(q_ref, f_ref, i_ref, g_ref, s0_ref, loglb_ref, log1mlb_ref, omlb_ref, gn_ref, m_ref,
                 o_ref, snew_ref, st_ref, *, c, heads, dk, n_valid, nlev):
    ci = pl.program_id(1)

    @pl.when(ci == 0)
    def _():
        for h in range(heads):
            st_ref[h] = s0_ref[h].T

    row = lax.broadcasted_iota(jnp.int32, (c, 1), 0)
    rr = lax.broadcasted_iota(jnp.int32, (c, c), 0)
    cc = lax.broadcasted_iota(jnp.int32, (c, c), 1)
    rxc = rr ^ cc
    mstack = m_ref[...]

    for h in range(heads):
        sl = slice(h * dk, (h + 1) * dk)
        zf = f_ref[:, sl]
        e = jnp.exp(-jnp.abs(zf))
        r = 1.0 / (1.0 + e)
        logsig = jnp.minimum(zf, 0.0) - jnp.log1p(e)
        a = loglb_ref[:, sl]
        b = log1mlb_ref[:, sl] + logsig
        log_f = jnp.maximum(a, b) + jnp.log1p(jnp.exp(-jnp.abs(a - b)))
        k_in = omlb_ref[:, sl] * jnp.where(zf >= 0, e * r, r)
        aq = q_ref[:, sl]
        q = aq * _sigmoid(aq)
        v = i_ref[:, sl]
        if n_valid < c:
            valid = row < n_valid
            log_f = jnp.where(valid, log_f, 0.0)
            k_in = jnp.where(valid, k_in, 0.0)

        ex = jnp.exp(jnp.dot(mstack, log_f, precision=HIGHEST, preferred_element_type=F32))
        cum_x = ex[nlev * c:(nlev + 1) * c]
        rem_x = ex[(nlev + 1) * c:(nlev + 2) * c]
        st = st_ref[h]
        o = lax.dot_general((q * cum_x).astype(BF16), st.astype(BF16), _NT, preferred_element_type=F32)

        att = jnp.where(rxc == 0, jnp.sum(q * k_in, axis=-1, keepdims=True), 0.0)
        for lev in range(nlev):
            half = 1 << lev
            xl = ex[lev * c:(lev + 1) * c]
            second = (row & half) != 0
            ql = jnp.where(second, q * xl, 0.0).astype(BF16)
            kl = jnp.where(second, 0.0, k_in * xl).astype(BF16)
            al = lax.dot_general(ql, kl, _NT, preferred_element_type=F32)
            att = att + jnp.where(rxc < 2 * half, al, 0.0)
        v16 = v.astype(BF16)
        o = o + jnp.dot(att.astype(BF16), v16, preferred_element_type=F32)

        kd = (k_in * rem_x).astype(BF16)
        st_ref[h] = st * cum_x[c - 1:c, :] + lax.dot_general(v16, kd, _TN, preferred_element_type=F32)

        ms = jnp.mean(o * o, axis=-1, keepdims=True)
        ag = g_ref[:, sl]
        o_ref[:, sl] = (o * lax.rsqrt(ms + EPS) * gn_ref[:, sl] * (ag * _sigmoid(ag))).astype(o_ref.dtype)

    @pl.when(ci == pl.num_programs(1) - 1)
    def _():
        for h in range(heads):
            snew_ref[h] = st_ref[h].T


def _hgrn(proj3, s0, loglb, log1mlb, omlb, gn_all, layer, c, n_valid, cols):
    bsz, lpad, _ = proj3.shape
    _, heads, dk, dv = s0.shape[-4:]
    assert dk == LANES and dv == LANES
    w = heads * dk
    mstack, nlev = _hgrn_level_matrix(c)
    mstack = jnp.asarray(mstack)
    cq, cf, ci_, cg = (x // w for x in cols)
    kern = functools.partial(_hgrn_kernel, c=c, heads=heads, dk=dk, n_valid=n_valid, nlev=nlev)
    s0_lead = s0.ndim - 4

    def s0_map(b, t):
        return ((layer,) if s0_lead == 1 else ()) + (b, 0, 0, 0)

    s0_block = ((None,) if s0_lead == 1 else ()) + (None, heads, dk, dv)

    def colspec(cb):
        return pl.BlockSpec((None, c, w), lambda b, t: (b, t, cb))

    def pspec():
        return pl.BlockSpec((None, 1, w), lambda b, t: (layer, 0, 0))

    return pl.pallas_call(
        kern,
        grid=(bsz, lpad // c),
        in_specs=[colspec(cq), colspec(cf), colspec(ci_), colspec(cg),
                  pl.BlockSpec(s0_block, s0_map),
                  pspec(), pspec(), pspec(), pspec(),
                  pl.BlockSpec(mstack.shape, lambda b, t: (0, 0))],
        out_specs=[pl.BlockSpec((None, c, w), lambda b, t: (b, t, 0)),
                   pl.BlockSpec((None, heads, dk, dv), lambda b, t: (b, 0, 0, 0))],
        out_shape=[jax.ShapeDtypeStruct((bsz, lpad, w), BF16),
                   jax.ShapeDtypeStruct((bsz, heads, dk, dv), F32)],
        scratch_shapes=[pltpu.VMEM((heads, dv, dk), F32)],
        compiler_params=_cparams(("parallel", "arbitrary")),
        name="hgrn",
    )(proj3, proj3, proj3, proj3, s0, loglb, log1mlb, omlb, gn_all, mstack)


def _lru_kernel(x_ref, g_ref, buf0_ref, h0_ref, cw_ref, cb_ref, wr_ref, br_ref, wi_ref, bi_ref, lam_ref,
                o_ref, hfin_ref, bufnew_ref, xp_ref, a_ref, u_ref, h_ref, *, tl, n_last, conv_w, nblk, bs):
    ti = pl.program_id(1)
    nt = pl.num_programs(1)
    pad = SUBLANES
    tail = conv_w - 1

    @pl.when(ti == 0)
    def _():
        xp_ref[0:pad, :] = jnp.zeros((pad, xp_ref.shape[1]), F32)
        xp_ref[pad - tail:pad, :] = buf0_ref[...]
        h_ref[...] = jnp.broadcast_to(h0_ref[...], h_ref.shape)

    xp_ref[pad:pad + tl, :] = x_ref[...]
    xc = cb_ref[...]
    for j in range(conv_w):
        xc = xc + xp_ref[pad - tail + j:pad - tail + j + tl, :] * cw_ref[j:j + 1, :]

    sp_lam = _softplus(-lam_ref[...])
    for n in range(nblk):
        sl = slice(n * bs, (n + 1) * bs)
        xb = xc[:, sl]
        xb16 = xb.astype(BF16)
        r = _sigmoid(jnp.dot(xb16, wr_ref[n], preferred_element_type=F32) + br_ref[:, sl])
        i = _sigmoid(jnp.dot(xb16, wi_ref[n], preferred_element_type=F32) + bi_ref[:, sl])
        log_a = -LRU_C * r * sp_lam[:, sl]
        a = jnp.exp(log_a)
        a_ref[0:tl, sl] = a
        u_ref[0:tl, sl] = jnp.sqrt(-jnp.tanh(log_a) * (a * a + 1.0)) * (i * xb)

    srow = lax.broadcasted_iota(jnp.int32, (SUBLANES, 1), 0)

    def tile_scan(base, hprev):
        a = a_ref[pl.ds(base, SUBLANES), :]
        u = u_ref[pl.ds(base, SUBLANES), :]
        for d in (1, 2, 4):
            a_s = jnp.where(srow >= d, pltpu.roll(a, d, 0), 1.0)
            u_s = jnp.where(srow >= d, pltpu.roll(u, d, 0), 0.0)
            u = u + a * u_s
            a = a * a_s
        hs = u + a * hprev
        a_ref[pl.ds(base, SUBLANES), :] = hs
        return hs

    n8 = -(-tl // SUBLANES)

    def body(k, hprev):
        base = pl.multiple_of(k * SUBLANES, SUBLANES)
        hs = tile_scan(base, hprev)
        return jnp.broadcast_to(hs[SUBLANES - 1:SUBLANES, :], hs.shape)

    if n8 > 1:
        hlast = lax.fori_loop(0, n8 - 1, body, h_ref[...])
    else:
        hlast = h_ref[...]
    hs_last = tile_scan((n8 - 1) * SUBLANES, hlast)
    lr_full = (tl - 1) % SUBLANES
    h_ref[...] = jnp.broadcast_to(hs_last[lr_full:lr_full + 1, :], h_ref.shape)

    gg = g_ref[...]
    o_ref[...] = (a_ref[0:tl, :] * (gg * _sigmoid(gg))).astype(o_ref.dtype)

    @pl.when(ti == nt - 1)
    def _():
        lr = (n_last - 1) % SUBLANES
        hfin_ref[...] = hs_last[lr:lr + 1, :]
        bufnew_ref[...] = xp_ref[pad + n_last - tail:pad + n_last, :]

    @pl.when(ti < nt - 1)
    def _():
        keep = xp_ref[tl:tl + pad, :]
        xp_ref[0:pad, :] = keep


def _lru(proj3, buf0, h0, cw_all, cb_all, wr_all, br_all, wi_all, bi_all, lam_all, layer, tl, n_last, cols):
    bsz, lpad, _ = proj3.shape
    conv_w, w = cw_all.shape[1:]
    nblk, bs = wr_all.shape[1:3]
    cx, cg = (x // w for x in cols)
    nt = lpad // tl
    assert n_last == tl or nt == 1
    tlp = -(-tl // SUBLANES) * SUBLANES
    kern = functools.partial(_lru_kernel, tl=tl, n_last=n_last, conv_w=conv_w, nblk=nblk, bs=bs)
    lead = buf0.ndim - 3

    def st_map(b, t):
        return ((layer,) if lead == 1 else ()) + (b, 0, 0)

    def st_block(rows):
        return ((None,) if lead == 1 else ()) + (None, rows, w)

    def pspec(rows):
        return pl.BlockSpec((None, rows, w), lambda b, t: (layer, 0, 0))

    def wspec():
        return pl.BlockSpec((None, nblk, bs, bs), lambda b, t: (layer, 0, 0, 0))

    return pl.pallas_call(
        kern,
        grid=(bsz, nt),
        in_specs=[pl.BlockSpec((None, tl, w), lambda b, t: (b, t, cx)),
                  pl.BlockSpec((None, tl, w), lambda b, t: (b, t, cg)),
                  pl.BlockSpec(st_block(conv_w - 1), st_map),
                  pl.BlockSpec(st_block(1), st_map),
                  pspec(conv_w), pspec(1), wspec(), pspec(1), wspec(), pspec(1), pspec(1)],
        out_specs=[pl.BlockSpec((None, tl, w), lambda b, t: (b, t, 0)),
                   pl.BlockSpec((None, 1, w), lambda b, t: (b, 0, 0)),
                   pl.BlockSpec((None, conv_w - 1, w), lambda b, t: (b, 0, 0))],
        out_shape=[jax.ShapeDtypeStruct((bsz, lpad, w), BF16),
                   jax.ShapeDtypeStruct((bsz, 1, w), F32),
                   jax.ShapeDtypeStruct((bsz, conv_w - 1, w), F32)],
        scratch_shapes=[pltpu.VMEM((tlp + 2 * SUBLANES, w), F32),
                        pltpu.VMEM((tlp, w), F32),
                        pltpu.VMEM((tlp, w), F32),
                        pltpu.VMEM((SUBLANES, w), F32)],
        compiler_params=_cparams(("parallel", "arbitrary")),
        name="lru",
    )(proj3, proj3, buf0, h0, cw_all, cb_all, wr_all, br_all, wi_all, bi_all, lam_all)


def _tri_ext(tk):
    s1 = np.arange(tk)[:, None]
    s0 = np.arange(tk)[None, :]
    return np.concatenate([(s1 > s0), np.ones((tk, tk), bool)], axis=1).astype(np.float32)


def _sb_block(z, mask, run, tri):
    tk = z.shape[1]
    sp = _softplus(z)
    lk = jnp.where(mask, -sp, 0.0)
    hi = lk.astype(BF16)
    lo = (lk - hi.astype(F32)).astype(BF16)
    ext = jnp.dot(hi, tri, preferred_element_type=F32) + jnp.dot(lo, tri, preferred_element_type=F32)
    after = ext[:, :tk] + run
    w = jnp.where(mask, jnp.exp(z - sp + after), 0.0)
    return w, run + ext[:, tk:]


def _sbp_kernel(bias_ref, q_ref, k_ref, v_ref, g_ref, tri_ref, o_ref, *, tq, dh):
    h = pl.program_id(1)
    qi = pl.program_id(2)
    scale = dh ** -0.5
    bias = bias_ref[h]
    q = (q_ref[...] * scale).astype(BF16)
    tri = tri_ref[...]
    rowg = qi * tq + lax.broadcasted_iota(jnp.int32, (tq, tq), 0)
    coll = lax.broadcasted_iota(jnp.int32, (tq, tq), 1)

    def body(i, carry):
        acc, run = carry
        kj = qi - i
        base = pl.multiple_of(kj * tq, tq)
        ks = k_ref[pl.ds(base, tq), :].astype(BF16)
        vs = v_ref[pl.ds(base, tq), :].astype(BF16)
        z = lax.dot_general(q, ks, _NT, preferred_element_type=F32) + bias
        mask = (base + coll) < rowg
        w, run = _sb_block(z, mask, run, tri)
        acc = acc + jnp.dot(w.astype(BF16), vs, preferred_element_type=F32)
        return acc, run

    acc, _ = lax.fori_loop(0, qi + 1, body, (jnp.zeros((tq, dh), F32), jnp.zeros((tq, tq), F32)))
    gg = g_ref[...]
    o_ref[...] = (acc * (gg * _sigmoid(gg))).astype(o_ref.dtype)


def _sb_prompt(proj3, bias_all, layer, heads, dh, tq, cols):
    bsz, seqlen, _ = proj3.shape
    cq, ck, cv, cg = (x // dh for x in cols)
    tri = jnp.asarray(_tri_ext(tq)).astype(BF16)
    kern = functools.partial(_sbp_kernel, tq=tq, dh=dh)
    return pl.pallas_call(
        kern,
        grid=(bsz, heads, seqlen // tq),
        in_specs=[pl.BlockSpec(memory_space=pltpu.SMEM),
                  pl.BlockSpec((None, tq, dh), lambda b, h, i: (b, i, cq + h)),
                  pl.BlockSpec((None, seqlen, dh), lambda b, h, i: (b, 0, ck + h)),
                  pl.BlockSpec((None, seqlen, dh), lambda b, h, i: (b, 0, cv + h)),
                  pl.BlockSpec((None, tq, dh), lambda b, h, i: (b, i, cg + h)),
                  pl.BlockSpec(tri.shape, lambda b, h, i: (0, 0))],
        out_specs=pl.BlockSpec((None, tq, dh), lambda b, h, i: (b, i, h)),
        out_shape=jax.ShapeDtypeStruct((bsz, seqlen, heads * dh), BF16),
        compiler_params=_cparams(("parallel", "parallel", "arbitrary")),
        name="sb_prompt",
    )(bias_all[layer], proj3, proj3, proj3, proj3, tri)


def _sbs_kernel(pt_ref, bias_ref, q_ref, kc_ref, vc_ref, g_ref, kp_ref, vp_ref, tri_ref, o_ref, acc_ref, run_ref,
                *, heads, dh, n_valid, page):
    j = pl.program_id(1)
    nr = SUBLANES
    scale = dh ** -0.5
    tri = tri_ref[...]
    rows = lax.broadcasted_iota(jnp.int32, (heads * nr, page), 0)
    colp = lax.broadcasted_iota(jnp.int32, (heads * nr, page), 1)
    trow = rows & (nr - 1)
    bias = jnp.concatenate([jnp.full((nr, page), bias_ref[h], F32) for h in range(heads)], axis=0)

    def sweep(get_k, get_v, mask, acc, run):
        z = jnp.concatenate(
            [lax.dot_general((q_ref[:, h * dh:(h + 1) * dh] * scale).astype(BF16), get_k(h), _NT,
                             preferred_element_type=F32) for h in range(heads)], axis=0) + bias
        w, run = _sb_block(z, mask, run, tri)
        w16 = w.astype(BF16)
        upd = jnp.concatenate(
            [jnp.dot(w16[h * nr:(h + 1) * nr], get_v(h), preferred_element_type=F32) for h in range(heads)], axis=0)
        return acc + upd, run

    @pl.when(j == 0)
    def _():
        zpad = jnp.zeros((page - nr, dh), F32)

        def cur_k(h):
            return jnp.concatenate([kc_ref[:, h * dh:(h + 1) * dh], zpad], axis=0).astype(BF16)

        def cur_v(h):
            return jnp.concatenate([vc_ref[:, h * dh:(h + 1) * dh], zpad], axis=0).astype(BF16)

        mask = (colp < trow) & (colp < n_valid)
        acc, run = sweep(cur_k, cur_v, mask, jnp.zeros((heads * nr, dh), F32), jnp.zeros((heads * nr, page), F32))
        acc_ref[...] = acc
        run_ref[...] = run

    def page_k(h):
        return kp_ref[pl.ds(h, page, stride=heads), :].astype(BF16)

    def page_v(h):
        return vp_ref[pl.ds(h, page, stride=heads), :].astype(BF16)

    acc, run = sweep(page_k, page_v, colp >= 0, acc_ref[...], run_ref[...])
    acc_ref[...] = acc
    run_ref[...] = run

    @pl.when(j == pl.num_programs(1) - 1)
    def _():
        for h in range(heads):
            gg = g_ref[:, h * dh:(h + 1) * dh]
            o_ref[:, h * dh:(h + 1) * dh] = (acc[h * nr:(h + 1) * nr] * (gg * _sigmoid(gg))).astype(o_ref.dtype)


def _sb_sample(proj3, cache_k, cache_v, page_table, bias_all, layer, n_valid, cols):
    bsz, lpad, _ = proj3.shape
    assert lpad == SUBLANES
    depth, n_phys, page, heads, dh = cache_k.shape
    n_pages = page_table.shape[1]
    w = heads * dh
    cq, ck, cv, cg = (x // w for x in cols)
    kp = cache_k.reshape(depth, n_phys, page * heads, dh)
    vp = cache_v.reshape(depth, n_phys, page * heads, dh)
    tri = jnp.asarray(_tri_ext(page)).astype(BF16)
    kern = functools.partial(_sbs_kernel, heads=heads, dh=dh, n_valid=n_valid, page=page)

    def colspec(cb):
        return pl.BlockSpec((None, lpad, w), lambda b, j, pt: (b, 0, cb))

    def pagespec():
        return pl.BlockSpec((None, None, page * heads, dh),
                            lambda b, j, pt: (layer, pt[b, n_pages - 1 - j], 0, 0))

    grid_spec = pltpu.PrefetchScalarGridSpec(
        num_scalar_prefetch=1,
        grid=(bsz, n_pages),
        in_specs=[pl.BlockSpec(memory_space=pltpu.SMEM),
                  colspec(cq), colspec(ck), colspec(cv), colspec(cg),
                  pagespec(), pagespec(),
                  pl.BlockSpec(tri.shape, lambda b, j, pt: (0, 0))],
        out_specs=pl.BlockSpec((None, lpad, w), lambda b, j, pt: (b, 0, 0)),
        scratch_shapes=[pltpu.VMEM((heads * SUBLANES, dh), F32), pltpu.VMEM((heads * SUBLANES, page), F32)],
    )
    return pl.pallas_call(
        kern,
        grid_spec=grid_spec,
        out_shape=jax.ShapeDtypeStruct((bsz, lpad, w), BF16),
        compiler_params=_cparams(("parallel", "arbitrary")),
        name="sb_sample",
    )(page_table, bias_all[layer], proj3, proj3, proj3, proj3, kp, vp, tri)


def _merge_kernel(oa_ref, ob_ref, oc_ref, ga_ref, gb_ref, gc_ref, x_ref, wa_ref, wb_ref, wc_ref, wo_ref, gp_ref,
                  y_ref):
    merged = (ga_ref[...].astype(F32) * jnp.dot(oa_ref[...], wa_ref[...], preferred_element_type=F32)
              + gb_ref[...].astype(F32) * jnp.dot(ob_ref[...], wb_ref[...], preferred_element_type=F32)
              + gc_ref[...].astype(F32) * jnp.dot(oc_ref[...], wc_ref[...], preferred_element_type=F32))
    out = jnp.dot(merged.astype(BF16), wo_ref[...], preferred_element_type=F32)
    ms = jnp.mean(out * out, axis=-1, keepdims=True)
    y_ref[...] = x_ref[...] + out * lax.rsqrt(ms + EPS) * gp_ref[...]


def _merge(oa, ob, oc, gates, x2d, wa_all, wb_all, wc_all, wo_all, gp_all, layer, bm):
    m, d = x2d.shape
    wbr = oa.shape[1]

    def ospec():
        return pl.BlockSpec((bm, wbr), lambda i: (i, 0))

    def gspec(k):
        return pl.BlockSpec((bm, d), lambda i: (i, k))

    def wspec(rows):
        return pl.BlockSpec((None, rows, d), lambda i: (layer, 0, 0), pipeline_mode=pl.Buffered(1))

    return pl.pallas_call(
        _merge_kernel,
        grid=(m // bm,),
        in_specs=[ospec(), ospec(), ospec(), gspec(0), gspec(1), gspec(2),
                  pl.BlockSpec((bm, d), lambda i: (i, 0)),
                  wspec(wbr), wspec(wbr), wspec(wbr), wspec(d),
                  pl.BlockSpec((None, 1, d), lambda i: (layer, 0, 0))],
        out_specs=pl.BlockSpec((bm, d), lambda i: (i, 0)),
        out_shape=jax.ShapeDtypeStruct((m, d), F32),
        compiler_params=_cparams(("parallel",)),
        name="merge",
    )(oa, ob, oc, gates, gates, gates, x2d, wa_all, wb_all, wc_all, wo_all, gp_all)


def _pick(n, prefs):
    for p in prefs:
        if n % p == 0:
            return p
    return n


def kernel(x_prompt, x_sample, cache_k, cache_v, state_hgrn, state_lru_h, state_conv, page_table, norm_pre, norm_post, w_in, hgrn_lb_logits, hgrn_norm, lru_conv_w, lru_conv_b, lru_w_r, lru_b_r, lru_w_i, lru_b_i, lru_lambda, sb_bias, w_branch_a, w_branch_b, w_branch_c, w_out):
    depth, d, n_cols = w_in.shape
    pb, seq, _ = x_prompt.shape
    sb, dec_seq, _ = x_sample.shape
    _, _, a_heads, a_dk, a_dv = state_hgrn.shape
    a_kw, a_w = a_heads * a_dk, a_heads * a_dv
    b_w = state_lru_h.shape[2]
    conv_w = lru_conv_w.shape[1]
    _, _, page, c_heads, c_dh = cache_k.shape
    c_w = c_heads * c_dh
    sizes = (a_kw, a_kw, a_w, a_w, b_w, b_w, c_w, c_w, c_w, c_w, d, d, d)
    assert sum(sizes) == n_cols and a_kw == a_w
    offs = np.concatenate([[0], np.cumsum(sizes)]).tolist()
    (o_aq, o_af, o_ai, o_ag, o_bx, o_bg, o_cq, o_ck, o_cv, o_cg, o_ma) = offs[:11]
    n_main = o_ma

    lbs = jnp.cumsum(jax.nn.softmax(hgrn_lb_logits.astype(F32), axis=0), axis=0)
    lbs = lbs - lbs[0:1]
    loglb = jnp.log(lbs)[:, None, :]
    log1mlb = jnp.log1p(-lbs)[:, None, :]
    omlb = (1.0 - lbs)[:, None, :]
    r3 = lambda p: p.astype(F32)[:, None, :]
    norm_pre3, norm_post3, hgrn_norm3 = r3(norm_pre), r3(norm_post), r3(hgrn_norm)
    conv_b3, b_r3, b_i3, lam3 = r3(lru_conv_b), r3(lru_b_r), r3(lru_b_i), r3(lru_lambda)
    conv_w3 = lru_conv_w.astype(F32)
    w_in16 = w_in.astype(BF16)
    w_r16, w_i16 = lru_w_r.astype(BF16), lru_w_i.astype(BF16)
    w_a16, w_b16, w_c16, w_o16 = (t.astype(BF16) for t in (w_branch_a, w_branch_b, w_branch_c, w_out))
    sbias = sb_bias.astype(F32)

    lpad = SUBLANES
    assert dec_seq <= lpad and dec_seq >= conv_w - 1
    xs_pad = jnp.pad(x_sample, ((0, 0), (0, lpad - dec_seq), (0, 0)))

    mp, ms = pb * seq, sb * lpad
    bn = _pick(n_main, (1024, 512, 256, 128))
    assert d % bn == 0
    hc = _pick(seq, (64, 32, 16, 8))
    tl = _pick(seq, (256, 128, 64, 32, 16, 8))
    tq = _pick(seq, (128,))
    hp, hs = x_prompt.reshape(mp, d), xs_pad.reshape(ms, d)
    zs_p = jnp.zeros((pb, a_heads, a_dk, a_dv), F32)
    zh_p = jnp.zeros((pb, 1, b_w), F32)
    zb_p = jnp.zeros((pb, conv_w - 1, b_w), F32)
    h0_s = state_lru_h[:, :, None, :]

    outs = [[] for _ in range(10)]
    for l in range(depth):
        for grp in range(2):
            if grp == 0:
                x2d, bsz, lp, nv = hp, pb, seq, seq
            else:
                x2d, bsz, lp, nv = hs, sb, lpad, dec_seq
            m = x2d.shape[0]
            bm = _pick(m, (1024, 512, 256, 128, 64, 32, 16, 8))
            main, gates = _inproj(x2d, norm_pre3, w_in16, l, n_main, bm, bn)
            main3 = main.reshape(bsz, lp, n_main)
            if grp == 0:
                o_a, s_new = _hgrn(main3, zs_p, loglb, log1mlb, omlb, hgrn_norm3, l, hc, hc, (o_aq, o_af, o_ai, o_ag))
                o_b, h_new, buf_new = _lru(main3, zb_p, zh_p, conv_w3, conv_b3, w_r16, b_r3, w_i16, b_i3, lam3, l,
                                           tl, tl, (o_bx, o_bg))
                o_c = _sb_prompt(main3, sbias, l, c_heads, c_dh, tq, (o_cq, o_ck, o_cv, o_cg))
            else:
                o_a, s_new = _hgrn(main3, state_hgrn, loglb, log1mlb, omlb, hgrn_norm3, l, lpad, nv,
                                   (o_aq, o_af, o_ai, o_ag))
                o_b, h_new, buf_new = _lru(main3, state_conv, h0_s, conv_w3, conv_b3, w_r16, b_r3, w_i16, b_i3,
                                           lam3, l, lpad, nv, (o_bx, o_bg))
                o_c = _sb_sample(main3, cache_k, cache_v, page_table, sbias, l, nv, (o_cq, o_ck, o_cv, o_cg))
            bmm = _pick(m, (256, 128, 64, 32, 16, 8))
            y = _merge(o_a.reshape(m, a_w), o_b.reshape(m, b_w), o_c.reshape(m, c_w), gates, x2d,
                       w_a16, w_b16, w_c16, w_o16, norm_post3, l, bmm)
            k_new = main3[:, :nv, o_ck:o_ck + c_w].reshape(bsz, nv, c_heads, c_dh)
            v_new = main3[:, :nv, o_cv:o_cv + c_w].reshape(bsz, nv, c_heads, c_dh)
            for lst, val in zip(outs[grp * 5:(grp + 1) * 5], (k_new, v_new, s_new, h_new[:, 0, :], buf_new)):
                lst.append(val)
            if grp == 0:
                hp = y
            else:
                hs = y
    y_p = hp.reshape(pb, seq, d)
    y_s = hs.reshape(sb, lpad, d)[:, :dec_seq]
    return (y_p, y_s) + tuple(jnp.stack(o) for o in outs)
```

```python
import functools
import math

import numpy as np
import jax
import jax.numpy as jnp
from jax import lax
from jax.experimental import pallas as pl
from jax.experimental.pallas import tpu as pltpu

F32 = jnp.float32
BF16 = jnp.bfloat16
EPS = 1e-6
LRU_C = 8.0
LOG2E = 1.4426950408889634
LANES = 128
SUBLANES = 8
VMEM_LIMIT = 56 * 1024 * 1024
HIGHEST = lax.Precision.HIGHEST

_NT = (((1,), (1,)), ((), ()))
_TN = (((0,), (0,)), ((), ()))


def _cparams(sem):
    return pltpu.CompilerParams(dimension_semantics=sem, vmem_limit_bytes=VMEM_LIMIT)


def _sigmoid(x):
    return 0.5 * jnp.tanh(0.5 * x) + 0.5


def _softplus(x):
    return jnp.maximum(x, 0.0) + jnp.log1p(jnp.exp(-jnp.abs(x)))


def _inproj_kernel(x_ref, g_ref, w_ref, proj_ref, xn_ref):
    @pl.when(pl.program_id(1) == 0)
    def _():
        x = x_ref[...]
        ms = jnp.mean(x * x, axis=-1, keepdims=True)
        xn_ref[...] = (x * lax.rsqrt(ms + EPS) * g_ref[...]).astype(BF16)

    proj_ref[...] = jnp.dot(xn_ref[...], w_ref[...], preferred_element_type=F32)


def _inproj(x2d, g_all, w_all, layer, bm, bn):
    m, d = x2d.shape
    n = w_all.shape[2]
    return pl.pallas_call(
        _inproj_kernel,
        grid=(m // bm, n // bn),
        in_specs=[pl.BlockSpec((bm, d), lambda i, j: (i, 0)),
                  pl.BlockSpec((None, 1, d), lambda i, j: (layer, 0, 0)),
                  pl.BlockSpec((None, d, bn), lambda i, j: (layer, 0, j))],
        out_specs=pl.BlockSpec((bm, bn), lambda i, j: (i, j)),
        out_shape=jax.ShapeDtypeStruct((m, n), F32),
        scratch_shapes=[pltpu.VMEM((bm, d), BF16)],
        compiler_params=_cparams(("parallel", "arbitrary")),
        name="inproj",
    )(x2d, g_all, w_all)


def _hgrn_level_matrix(c):
    nlev = int(math.log2(c))
    assert 1 << nlev == c
    t = np.arange(c)[:, None]
    j = np.arange(c)[None, :]
    mats = []
    for lev in range(nlev):
        m = 2 << lev
        mid = (t // m) * m + m // 2 - 1
        second = (t % m) >= m // 2
        q_side = second & (j > mid) & (j <= t)
        k_side = (~second) & (j > t) & (j <= mid)
        mats.append(q_side | k_side)
    mats.append(j <= t)
    mats.append(j > t)
    return np.concatenate(mats, axis=0).astype(np.float32), nlev


def _hgrn_kernel(q_ref, f_ref, i_ref, g_ref, s0_ref, loglb_ref, log1mlb_ref, omlb_ref, gn_ref, m_ref,
                 o_ref, snew_ref, st_ref, *, c, heads, dk, n_valid, nlev, gs):
    ci = pl.program_id(1)

    @pl.when(ci == 0)
    def _():
        for g in range(gs):
            for h in range(heads):
                st_ref[g, h] = s0_ref[g, h].T

    row = lax.broadcasted_iota(jnp.int32, (c, 1), 0)
    rr = lax.broadcasted_iota(jnp.int32, (c, c), 0)
    cc = lax.broadcasted_iota(jnp.int32, (c, c), 1)
    rxc = rr ^ cc
    mstack = m_ref[...]

    for g, h in [(g, h) for g in range(gs) for h in range(heads)]:
        sl = slice(h * dk, (h + 1) * dk)
        zf = f_ref[g, :, sl]
        e = jnp.exp(-jnp.abs(zf))
        r = 1.0 / (1.0 + e)
        logsig = jnp.minimum(zf, 0.0) - jnp.log1p(e)
        a = loglb_ref[:, sl]
        b = log1mlb_ref[:, sl] + logsig
        log_f = jnp.maximum(a, b) + jnp.log1p(jnp.exp(-jnp.abs(a - b)))
        k_in = omlb_ref[:, sl] * jnp.where(zf >= 0, e * r, r)
        aq = q_ref[g, :, sl]
        q = aq * _sigmoid(aq)
        v = i_ref[g, :, sl]
        if n_valid < c:
            valid = row < n_valid
            log_f = jnp.where(valid, log_f, 0.0)
            k_in = jnp.where(valid, k_in, 0.0)

        lf_hi = log_f.astype(BF16)
        lf_lo = (log_f - lf_hi.astype(F32)).astype(BF16)
        ex = jnp.exp(jnp.dot(mstack, lf_hi, preferred_element_type=F32)
                     + jnp.dot(mstack, lf_lo, preferred_element_type=F32))
        cum_x = ex[nlev * c:(nlev + 1) * c]
        rem_x = ex[(nlev + 1) * c:(nlev + 2) * c]
        st = st_ref[g, h]
        o = lax.dot_general((q * cum_x).astype(BF16), st.astype(BF16), _NT, preferred_element_type=F32)

        att = jnp.where(rxc == 0, jnp.sum(q * k_in, axis=-1, keepdims=True), 0.0)
        for lev in range(nlev):
            half = 1 << lev
            xl = ex[lev * c:(lev + 1) * c]
            second = (row & half) != 0
            ql = jnp.where(second, q * xl, 0.0).astype(BF16)
            kl = jnp.where(second, 0.0, k_in * xl).astype(BF16)
            al = lax.dot_general(ql, kl, _NT, preferred_element_type=F32)
            att = att + jnp.where(rxc < 2 * half, al, 0.0)
        v16 = v.astype(BF16)
        o = o + jnp.dot(att.astype(BF16), v16, preferred_element_type=F32)

        kd = (k_in * rem_x).astype(BF16)
        st_ref[g, h] = st * cum_x[c - 1:c, :] + lax.dot_general(v16, kd, _TN, preferred_element_type=F32)

        ms = jnp.mean(o * o, axis=-1, keepdims=True)
        ag = g_ref[g, :, sl]
        o_ref[g, :, sl] = (o * lax.rsqrt(ms + EPS) * gn_ref[:, sl] * (ag * _sigmoid(ag))).astype(o_ref.dtype)

    @pl.when(ci == pl.num_programs(1) - 1)
    def _():
        for g in range(gs):
            for h in range(heads):
                snew_ref[g, h] = st_ref[g, h].T


def _hgrn(proj3, s0, loglb, log1mlb, omlb, gn_all, layer, c, n_valid, gs, cols):
    bsz, lpad, _ = proj3.shape
    _, heads, dk, dv = s0.shape[-4:]
    assert dk == LANES and dv == LANES and bsz % gs == 0
    w = heads * dk
    mstack, nlev = _hgrn_level_matrix(c)
    mstack = jnp.asarray(mstack).astype(BF16)
    cq, cf, ci_, cg = (x // w for x in cols)
    kern = functools.partial(_hgrn_kernel, c=c, heads=heads, dk=dk, n_valid=n_valid, nlev=nlev, gs=gs)
    s0_lead = s0.ndim - 4

    def s0_map(b, t):
        return ((layer,) if s0_lead == 1 else ()) + (b, 0, 0, 0)

    s0_block = ((None,) if s0_lead == 1 else ()) + (gs, heads, dk, dv)

    def colspec(cb):
        return pl.BlockSpec((gs, c, w), lambda b, t: (b, t, cb))

    def pspec():
        return pl.BlockSpec((None, 1, w), lambda b, t: (layer, 0, 0))

    return pl.pallas_call(
        kern,
        grid=(bsz // gs, lpad // c),
        in_specs=[colspec(cq), colspec(cf), colspec(ci_), colspec(cg),
                  pl.BlockSpec(s0_block, s0_map),
                  pspec(), pspec(), pspec(), pspec(),
                  pl.BlockSpec(mstack.shape, lambda b, t: (0, 0))],
        out_specs=[pl.BlockSpec((gs, c, w), lambda b, t: (b, t, 0)),
                   pl.BlockSpec((gs, heads, dk, dv), lambda b, t: (b, 0, 0, 0))],
        out_shape=[jax.ShapeDtypeStruct((bsz, lpad, w), BF16),
                   jax.ShapeDtypeStruct((bsz, heads, dk, dv), F32)],
        scratch_shapes=[pltpu.VMEM((gs, heads, dv, dk), F32)],
        compiler_params=_cparams(("parallel", "arbitrary")),
        name="hgrn",
    )(proj3, proj3, proj3, proj3, s0, loglb, log1mlb, omlb, gn_all, mstack)


def _lru_kernel(x_ref, g_ref, buf0_ref, h0_ref, cw_ref, cb_ref, wr_ref, br_ref, wi_ref, bi_ref, lam_ref,
                o_ref, hfin_ref, bufnew_ref, xp_ref, a_ref, u_ref, h_ref, *, tl, n_last, conv_w, nblk, bs):
    ti = pl.program_id(1)
    nt = pl.num_programs(1)
    pad = SUBLANES
    tail = conv_w - 1

    @pl.when(ti == 0)
    def _():
        xp_ref[0:pad, :] = jnp.zeros((pad, xp_ref.shape[1]), F32)
        xp_ref[pad - tail:pad, :] = buf0_ref[...]
        h_ref[...] = jnp.broadcast_to(h0_ref[...], h_ref.shape)

    xp_ref[pad:pad + tl, :] = x_ref[...]
    xc = cb_ref[...]
    for j in range(conv_w):
        xc = xc + xp_ref[pad - tail + j:pad - tail + j + tl, :] * cw_ref[j:j + 1, :]

    sp_lam = _softplus(-lam_ref[...])
    for n in range(nblk):
        sl = slice(n * bs, (n + 1) * bs)
        xb = xc[:, sl]
        xb16 = xb.astype(BF16)
        r = _sigmoid(jnp.dot(xb16, wr_ref[n], preferred_element_type=F32) + br_ref[:, sl])
        i = _sigmoid(jnp.dot(xb16, wi_ref[n], preferred_element_type=F32) + bi_ref[:, sl])
        log_a = -LRU_C * r * sp_lam[:, sl]
        a = jnp.exp(log_a)
        a_ref[0:tl, sl] = a
        u_ref[0:tl, sl] = jnp.sqrt(-jnp.tanh(log_a) * (a * a + 1.0)) * (i * xb)

    srow = lax.broadcasted_iota(jnp.int32, (SUBLANES, 1), 0)

    def tile_scan(base, hprev):
        a = a_ref[pl.ds(base, SUBLANES), :]
        u = u_ref[pl.ds(base, SUBLANES), :]
        for d in (1, 2, 4):
            a_s = jnp.where(srow >= d, pltpu.roll(a, d, 0), 1.0)
            u_s = jnp.where(srow >= d, pltpu.roll(u, d, 0), 0.0)
            u = u + a * u_s
            a = a * a_s
        hs = u + a * hprev
        a_ref[pl.ds(base, SUBLANES), :] = hs
        return hs

    n8 = -(-tl // SUBLANES)

    def body(k, hprev):
        base = pl.multiple_of(k * SUBLANES, SUBLANES)
        hs = tile_scan(base, hprev)
        return jnp.broadcast_to(hs[SUBLANES - 1:SUBLANES, :], hs.shape)

    if n8 > 1:
        hlast = lax.fori_loop(0, n8 - 1, body, h_ref[...])
    else:
        hlast = h_ref[...]
    hs_last = tile_scan((n8 - 1) * SUBLANES, hlast)
    lr_full = (tl - 1) % SUBLANES
    h_ref[...] = jnp.broadcast_to(hs_last[lr_full:lr_full + 1, :], h_ref.shape)

    gg = g_ref[...]
    o_ref[...] = (a_ref[0:tl, :] * (gg * _sigmoid(gg))).astype(o_ref.dtype)

    @pl.when(ti == nt - 1)
    def _():
        lr = (n_last - 1) % SUBLANES
        hfin_ref[...] = hs_last[lr:lr + 1, :]
        bufnew_ref[...] = xp_ref[pad + n_last - tail:pad + n_last, :]

    @pl.when(ti < nt - 1)
    def _():
        keep = xp_ref[tl:tl + pad, :]
        xp_ref[0:pad, :] = keep


def _lru(proj3, buf0, h0, cw_all, cb_all, wr_all, br_all, wi_all, bi_all, lam_all, layer, tl, n_last, cols):
    bsz, lpad, _ = proj3.shape
    conv_w, w = cw_all.shape[1:]
    nblk, bs = wr_all.shape[1:3]
    cx, cg = (x // w for x in cols)
    nt = lpad // tl
    assert n_last == tl or nt == 1
    tlp = -(-tl // SUBLANES) * SUBLANES
    kern = functools.partial(_lru_kernel, tl=tl, n_last=n_last, conv_w=conv_w, nblk=nblk, bs=bs)
    lead = buf0.ndim - 3

    def st_map(b, t):
        return ((layer,) if lead == 1 else ()) + (b, 0, 0)

    def st_block(rows):
        return ((None,) if lead == 1 else ()) + (None, rows, w)

    def pspec(rows):
        return pl.BlockSpec((None, rows, w), lambda b, t: (layer, 0, 0))

    def wspec():
        return pl.BlockSpec((None, nblk, bs, bs), lambda b, t: (layer, 0, 0, 0))

    return pl.pallas_call(
        kern,
        grid=(bsz, nt),
        in_specs=[pl.BlockSpec((None, tl, w), lambda b, t: (b, t, cx)),
                  pl.BlockSpec((None, tl, w), lambda b, t: (b, t, cg)),
                  pl.BlockSpec(st_block(conv_w - 1), st_map),
                  pl.BlockSpec(st_block(1), st_map),
                  pspec(conv_w), pspec(1), wspec(), pspec(1), wspec(), pspec(1), pspec(1)],
        out_specs=[pl.BlockSpec((None, tl, w), lambda b, t: (b, t, 0)),
                   pl.BlockSpec((None, 1, w), lambda b, t: (b, 0, 0)),
                   pl.BlockSpec((None, conv_w - 1, w), lambda b, t: (b, 0, 0))],
        out_shape=[jax.ShapeDtypeStruct((bsz, lpad, w), BF16),
                   jax.ShapeDtypeStruct((bsz, 1, w), F32),
                   jax.ShapeDtypeStruct((bsz, conv_w - 1, w), F32)],
        scratch_shapes=[pltpu.VMEM((tlp + 2 * SUBLANES, w), F32),
                        pltpu.VMEM((tlp, w), F32),
                        pltpu.VMEM((tlp, w), F32),
                        pltpu.VMEM((SUBLANES, w), F32)],
        compiler_params=_cparams(("parallel", "arbitrary")),
        name="lru",
    )(proj3, proj3, buf0, h0, cw_all, cb_all, wr_all, br_all, wi_all, bi_all, lam_all)


def _tri_ext(tk):
    s1 = np.arange(tk)[:, None]
    s0 = np.arange(tk)[None, :]
    return np.concatenate([(s1 > s0), np.ones((tk, tk), bool)], axis=1).astype(np.float32)


def _sb_terms(z2):
    mx = jnp.maximum(z2, 0.0)
    mn = jnp.minimum(z2, 0.0)
    t = jnp.log2(1.0 + jnp.exp2(mn - mx))
    return t + mx, mn - t


def _sbp_kernel(bias_ref, q_ref, k_ref, v_ref, g_ref, tri_ref, o_ref, k16_ref, v16_ref, *, tq, dh, nq, group, unroll):
    h = pl.program_id(1)
    qscale = dh ** -0.5 * LOG2E
    bias2 = bias_ref[h] * LOG2E
    tri = tri_ref[...]
    for j in range(nq):
        p = nq - 1 - j
        k16_ref[p * tq:(p + 1) * tq, :] = k_ref[j * tq:(j + 1) * tq, :].astype(BF16)
        v16_ref[p * tq:(p + 1) * tq, :] = v_ref[j * tq:(j + 1) * tq, :].astype(BF16)
    cmr = lax.broadcasted_iota(jnp.int32, (tq, tq), 1) - lax.broadcasted_iota(jnp.int32, (tq, tq), 0)

    def q_block(qi, first, nkb):
        r0 = pl.multiple_of(qi * tq, tq)
        lo = (nq - nkb) * tq
        q = (q_ref[pl.ds(r0, tq), :] * qscale).astype(BF16)
        z2 = lax.dot_general(q, k16_ref[lo:nq * tq, :], _NT, preferred_element_type=F32) + bias2
        sp, ls = _sb_terms(z2)
        masks = {}
        sp_blocks = []
        for p in range(nkb):
            j = nkb - 1 - p
            blk = sp[:, p * tq:(p + 1) * tq]
            if j >= first:
                masks[p] = cmr < (qi - j) * tq
                blk = jnp.where(masks[p], blk, 0.0)
            sp_blocks.append(blk.astype(BF16))
        ext = jnp.dot(jnp.concatenate(sp_blocks, axis=0), tri, preferred_element_type=F32)
        run = jnp.zeros((tq, tq), F32)
        w_blocks = []
        for p in range(nkb):
            e = ext[p * tq:(p + 1) * tq]
            wp = jnp.exp2(ls[:, p * tq:(p + 1) * tq] - e[:, :tq] - run)
            if p in masks:
                wp = jnp.where(masks[p], wp, 0.0)
            w_blocks.append(wp.astype(BF16))
            run = run + e[:, tq:]
        acc = jnp.dot(jnp.concatenate(w_blocks, axis=1), v16_ref[lo:nq * tq, :], preferred_element_type=F32)
        gg = g_ref[pl.ds(r0, tq), :]
        o_ref[pl.ds(r0, tq), :] = (acc * (gg * _sigmoid(gg))).astype(o_ref.dtype)

    for first in range(0, nq, group):
        nkb = first + group

        def body(i, carry, first=first, nkb=nkb):
            for u in range(unroll):
                q_block(first + i * unroll + u, first, nkb)
            return carry

        lax.fori_loop(0, group // unroll, body, 0)


def _sb_prompt(proj3, bias_all, layer, heads, dh, tq, group, unroll, cols):
    bsz, seqlen, _ = proj3.shape
    nq = seqlen // tq
    assert nq % group == 0 and group % unroll == 0
    cq, ck, cv, cg = (x // dh for x in cols)
    tri = jnp.asarray(_tri_ext(tq)).astype(BF16)
    kern = functools.partial(_sbp_kernel, tq=tq, dh=dh, nq=nq, group=group, unroll=unroll)

    def colspec(c0):
        return pl.BlockSpec((None, seqlen, dh), lambda b, h: (b, 0, c0 + h))

    return pl.pallas_call(
        kern,
        grid=(bsz, heads),
        in_specs=[pl.BlockSpec(memory_space=pltpu.SMEM),
                  colspec(cq), colspec(ck), colspec(cv), colspec(cg),
                  pl.BlockSpec(tri.shape, lambda b, h: (0, 0))],
        out_specs=pl.BlockSpec((None, seqlen, dh), lambda b, h: (b, 0, h)),
        out_shape=jax.ShapeDtypeStruct((bsz, seqlen, heads * dh), BF16),
        scratch_shapes=[pltpu.VMEM((seqlen, dh), BF16), pltpu.VMEM((seqlen, dh), BF16)],
        compiler_params=_cparams(("parallel", "parallel")),
        name="sb_prompt",
    )(bias_all[layer], proj3, proj3, proj3, proj3, tri)


def _sbs_kernel(pt_ref, bias_ref, q_ref, kc_ref, vc_ref, g_ref, *rest, heads, dh, n_valid, page, pps):
    kp_refs, vp_refs = rest[:pps], rest[pps:2 * pps]
    tri_ref, o_ref, qbd_ref, k16_ref, v16_ref, acc_ref, run_ref = rest[2 * pps:]
    j = pl.program_id(1)
    nr = SUBLANES
    hr = heads * nr
    w = heads * dh
    tri = tri_ref[...]
    bias2 = jnp.concatenate([jnp.full((nr, page), bias_ref[h] * LOG2E, F32) for h in range(heads)], axis=0)

    def sweep(qbd, k16, v16, mask, acc, run):
        nb = k16.shape[0] // page
        z2 = lax.dot_general(qbd, k16, _NT, preferred_element_type=F32)
        sps, lss = [], []
        for p in range(nb):
            sp, ls = _sb_terms(z2[:, p * page:(p + 1) * page] + bias2)
            if mask is not None:
                sp = jnp.where(mask, sp, 0.0)
            sps.append(sp.astype(BF16))
            lss.append(ls)
        ext = jnp.dot(jnp.concatenate(sps, axis=0), tri, preferred_element_type=F32)
        ws = []
        for p in range(nb):
            e = ext[p * hr:(p + 1) * hr]
            wp = jnp.exp2(lss[p] - e[:, :page] - run)
            if mask is not None:
                wp = jnp.where(mask, wp, 0.0)
            ws.append(wp.astype(BF16))
            run = run + e[:, page:]
        return acc + jnp.dot(jnp.concatenate(ws, axis=1), v16, preferred_element_type=F32), run

    @pl.when(j == 0)
    def _():
        rh = lax.broadcasted_iota(jnp.int32, (hr, w), 0) // nr
        ch = lax.broadcasted_iota(jnp.int32, (hr, w), 1) // dh
        qt = jnp.concatenate([q_ref[...] * (dh ** -0.5 * LOG2E)] * heads, axis=0)
        qbd_ref[...] = jnp.where(rh == ch, qt, 0.0).astype(BF16)
        zpad = jnp.zeros((page - nr, w), F32)
        colp = lax.broadcasted_iota(jnp.int32, (hr, page), 1)
        trow = lax.broadcasted_iota(jnp.int32, (hr, page), 0) & (nr - 1)
        kcur = jnp.concatenate([kc_ref[...], zpad], axis=0).astype(BF16)
        vcur = jnp.concatenate([vc_ref[...], zpad], axis=0).astype(BF16)
        mask = (colp < trow) & (colp < n_valid)
        acc, run = sweep(qbd_ref[...], kcur, vcur, mask, jnp.zeros((hr, w), F32), jnp.zeros((hr, page), F32))
        acc_ref[...] = acc
        run_ref[...] = run

    for p in range(pps):
        k16_ref[p * page:(p + 1) * page, :] = kp_refs[p][...].astype(BF16)
        v16_ref[p * page:(p + 1) * page, :] = vp_refs[p][...].astype(BF16)
    acc, run = sweep(qbd_ref[...], k16_ref[...], v16_ref[...], None, acc_ref[...], run_ref[...])
    acc_ref[...] = acc
    run_ref[...] = run

    @pl.when(j == pl.num_programs(1) - 1)
    def _():
        for h in range(heads):
            sl = slice(h * dh, (h + 1) * dh)
            gg = g_ref[:, sl]
            o_ref[:, sl] = (acc[h * nr:(h + 1) * nr, sl] * (gg * _sigmoid(gg))).astype(o_ref.dtype)


def _sb_sample(proj3, cache_k, cache_v, page_table, bias_all, layer, n_valid, pps, cols):
    bsz, lpad, _ = proj3.shape
    assert lpad == SUBLANES
    depth, n_phys, page, heads, dh = cache_k.shape
    n_pages = page_table.shape[1]
    assert n_pages % pps == 0
    w = heads * dh
    cq, ck, cv, cg = (x // w for x in cols)
    kp = cache_k.reshape(depth, n_phys, page, w)
    vp = cache_v.reshape(depth, n_phys, page, w)
    tri = jnp.asarray(_tri_ext(page)).astype(BF16)
    kern = functools.partial(_sbs_kernel, heads=heads, dh=dh, n_valid=n_valid, page=page, pps=pps)

    def colspec(cb):
        return pl.BlockSpec((None, lpad, w), lambda b, j, pt: (b, 0, cb))

    def pagespec(p):
        return pl.BlockSpec((None, None, page, w),
                            lambda b, j, pt: (layer, pt[b, n_pages - 1 - (j * pps + p)], 0, 0))

    grid_spec = pltpu.PrefetchScalarGridSpec(
        num_scalar_prefetch=1,
        grid=(bsz, n_pages // pps),
        in_specs=[pl.BlockSpec(memory_space=pltpu.SMEM),
                  colspec(cq), colspec(ck), colspec(cv), colspec(cg)]
                 + [pagespec(p) for p in range(pps)] + [pagespec(p) for p in range(pps)]
                 + [pl.BlockSpec(tri.shape, lambda b, j, pt: (0, 0))],
        out_specs=pl.BlockSpec((None, lpad, w), lambda b, j, pt: (b, 0, 0)),
        scratch_shapes=[pltpu.VMEM((heads * SUBLANES, w), BF16),
                        pltpu.VMEM((pps * page, w), BF16), pltpu.VMEM((pps * page, w), BF16),
                        pltpu.VMEM((heads * SUBLANES, w), F32), pltpu.VMEM((heads * SUBLANES, page), F32)],
    )
    return pl.pallas_call(
        kern,
        grid_spec=grid_spec,
        out_shape=jax.ShapeDtypeStruct((bsz, lpad, w), BF16),
        compiler_params=_cparams(("parallel", "arbitrary")),
        name="sb_sample",
    )(page_table, bias_all[layer], proj3, proj3, proj3, proj3, *([kp] * pps), *([vp] * pps), tri)


def _merge_kernel(oa_ref, ob_ref, oc_ref, ga_ref, gb_ref, gc_ref, x_ref, wa_ref, wb_ref, wc_ref, wo_ref, gp_ref,
                  y_ref):
    merged = (_sigmoid(ga_ref[...]) * jnp.dot(oa_ref[...], wa_ref[...], preferred_element_type=F32)
              + _sigmoid(gb_ref[...]) * jnp.dot(ob_ref[...], wb_ref[...], preferred_element_type=F32)
              + _sigmoid(gc_ref[...]) * jnp.dot(oc_ref[...], wc_ref[...], preferred_element_type=F32))
    out = jnp.dot(merged.astype(BF16), wo_ref[...], preferred_element_type=F32)
    ms = jnp.mean(out * out, axis=-1, keepdims=True)
    y_ref[...] = x_ref[...] + out * lax.rsqrt(ms + EPS) * gp_ref[...]


def _merge(oa, ob, oc, proj, gate_col, x2d, wa_all, wb_all, wc_all, wo_all, gp_all, layer, bm):
    m, d = x2d.shape
    wbr = oa.shape[1]
    assert gate_col % d == 0

    def ospec():
        return pl.BlockSpec((bm, wbr), lambda i: (i, 0))

    def gspec(k):
        return pl.BlockSpec((bm, d), lambda i: (i, gate_col // d + k))

    def wspec(rows):
        return pl.BlockSpec((None, rows, d), lambda i: (layer, 0, 0), pipeline_mode=pl.Buffered(1))

    return pl.pallas_call(
        _merge_kernel,
        grid=(m // bm,),
        in_specs=[ospec(), ospec(), ospec(), gspec(0), gspec(1), gspec(2),
                  pl.BlockSpec((bm, d), lambda i: (i, 0)),
                  wspec(wbr), wspec(wbr), wspec(wbr), wspec(d),
                  pl.BlockSpec((None, 1, d), lambda i: (layer, 0, 0))],
        out_specs=pl.BlockSpec((bm, d), lambda i: (i, 0)),
        out_shape=jax.ShapeDtypeStruct((m, d), F32),
        compiler_params=_cparams(("parallel",)),
        name="merge",
    )(oa, ob, oc, proj, proj, proj, x2d, wa_all, wb_all, wc_all, wo_all, gp_all)


def _pick(n, prefs):
    for p in prefs:
        if n % p == 0:
            return p
    return n


def kernel(x_prompt, x_sample, cache_k, cache_v, state_hgrn, state_lru_h, state_conv, page_table, norm_pre, norm_post, w_in, hgrn_lb_logits, hgrn_norm, lru_conv_w, lru_conv_b, lru_w_r, lru_b_r, lru_w_i, lru_b_i, lru_lambda, sb_bias, w_branch_a, w_branch_b, w_branch_c, w_out):
    depth, d, n_cols = w_in.shape
    pb, seq, _ = x_prompt.shape
    sb, dec_seq, _ = x_sample.shape
    _, _, a_heads, a_dk, a_dv = state_hgrn.shape
    a_kw, a_w = a_heads * a_dk, a_heads * a_dv
    b_w = state_lru_h.shape[2]
    conv_w = lru_conv_w.shape[1]
    _, _, page, c_heads, c_dh = cache_k.shape
    c_w = c_heads * c_dh
    sizes = (a_kw, a_kw, a_w, a_w, b_w, b_w, c_w, c_w, c_w, c_w, d, d, d)
    assert sum(sizes) == n_cols and a_kw == a_w
    offs = np.concatenate([[0], np.cumsum(sizes)]).tolist()
    (o_aq, o_af, o_ai, o_ag, o_bx, o_bg, o_cq, o_ck, o_cv, o_cg, o_ma) = offs[:11]

    lbs = jnp.cumsum(jax.nn.softmax(hgrn_lb_logits.astype(F32), axis=0), axis=0)
    lbs = lbs - lbs[0:1]
    loglb = jnp.log(lbs)[:, None, :]
    log1mlb = jnp.log1p(-lbs)[:, None, :]
    omlb = (1.0 - lbs)[:, None, :]
    r3 = lambda p: p.astype(F32)[:, None, :]
    norm_pre3, norm_post3, hgrn_norm3 = r3(norm_pre), r3(norm_post), r3(hgrn_norm)
    conv_b3, b_r3, b_i3, lam3 = r3(lru_conv_b), r3(lru_b_r), r3(lru_b_i), r3(lru_lambda)
    conv_w3 = lru_conv_w.astype(F32)
    w_in16 = w_in.astype(BF16)
    w_r16, w_i16 = lru_w_r.astype(BF16), lru_w_i.astype(BF16)
    w_a16, w_b16, w_c16, w_o16 = (t.astype(BF16) for t in (w_branch_a, w_branch_b, w_branch_c, w_out))
    sbias = sb_bias.astype(F32)

    lpad = SUBLANES
    assert dec_seq <= lpad and dec_seq >= conv_w - 1
    xs_pad = jnp.pad(x_sample, ((0, 0), (0, lpad - dec_seq), (0, 0)))

    mp, ms = pb * seq, sb * lpad
    bn = _pick(n_cols, (1024, 512, 256, 128))
    hc = _pick(seq, (128, 64, 32, 16, 8))
    tl = _pick(seq, (256, 128, 64, 32, 16, 8))
    tq = _pick(seq, (128,))
    sbg = _pick(seq // tq, (2, 1))
    sbu = sbg
    hgs = _pick(sb, (4, 2, 1))
    pps = _pick(page_table.shape[1], (8, 4, 2, 1))
    hp, hs = x_prompt.reshape(mp, d), xs_pad.reshape(ms, d)
    zs_p = jnp.zeros((pb, a_heads, a_dk, a_dv), F32)
    zh_p = jnp.zeros((pb, 1, b_w), F32)
    zb_p = jnp.zeros((pb, conv_w - 1, b_w), F32)
    h0_s = state_lru_h[:, :, None, :]

    outs = [[] for _ in range(10)]
    for l in range(depth):
        for grp in range(2):
            if grp == 0:
                x2d, bsz, lp, nv = hp, pb, seq, seq
            else:
                x2d, bsz, lp, nv = hs, sb, lpad, dec_seq
            m = x2d.shape[0]
            bm = _pick(m, (1024, 512, 256, 128, 64, 32, 16, 8))
            proj = _inproj(x2d, norm_pre3, w_in16, l, bm, bn)
            main3 = proj.reshape(bsz, lp, n_cols)
            if grp == 0:
                o_a, s_new = _hgrn(main3, zs_p, loglb, log1mlb, omlb, hgrn_norm3, l, hc, hc, 1,
                                   (o_aq, o_af, o_ai, o_ag))
                o_b, h_new, buf_new = _lru(main3, zb_p, zh_p, conv_w3, conv_b3, w_r16, b_r3, w_i16, b_i3, lam3, l,
                                           tl, tl, (o_bx, o_bg))
                o_c = _sb_prompt(main3, sbias, l, c_heads, c_dh, tq, sbg, sbu, (o_cq, o_ck, o_cv, o_cg))
            else:
                o_a, s_new = _hgrn(main3, state_hgrn, loglb, log1mlb, omlb, hgrn_norm3, l, lpad, nv, hgs,
                                   (o_aq, o_af, o_ai, o_ag))
                o_b, h_new, buf_new = _lru(main3, state_conv, h0_s, conv_w3, conv_b3, w_r16, b_r3, w_i16, b_i3,
                                           lam3, l, lpad, nv, (o_bx, o_bg))
                o_c = _sb_sample(main3, cache_k, cache_v, page_table, sbias, l, nv, pps, (o_cq, o_ck, o_cv, o_cg))
            bmm = _pick(m, (256, 128, 64, 32, 16, 8))
            y = _merge(o_a.reshape(m, a_w), o_b.reshape(m, b_w), o_c.reshape(m, c_w), proj, o_ma, x2d,
                       w_a16, w_b16, w_c16, w_o16, norm_post3, l, bmm)
            k_new = main3[:, :nv, o_ck:o_ck + c_w].reshape(bsz, nv, c_heads, c_dh)
            v_new = main3[:, :nv, o_cv:o_cv + c_w].reshape(bsz, nv, c_heads, c_dh)
            for lst, val in zip(outs[grp * 5:(grp + 1) * 5], (k_new, v_new, s_new, h_new[:, 0, :], buf_new)):
                lst.append(val)
            if grp == 0:
                hp = y
            else:
                hs = y
    y_p = hp.reshape(pb, seq, d)
    y_s = hs.reshape(sb, lpad, d)[:, :dec_seq]
    return (y_p, y_s) + tuple(jnp.stack(o) for o in outs)
```

```python
import functools
import math

import numpy as np
import jax
import jax.numpy as jnp
from jax import lax
from jax.experimental import pallas as pl
from jax.experimental.pallas import tpu as pltpu

F32 = jnp.float32
BF16 = jnp.bfloat16
EPS = 1e-6
LRU_C = 8.0
LOG2E = 1.4426950408889634
LANES = 128
SUBLANES = 8
VMEM_LIMIT = 56 * 1024 * 1024

_NT = (((1,), (1,)), ((), ()))
_TN = (((0,), (0,)), ((), ()))


def _cparams(sem):
    return pltpu.CompilerParams(dimension_semantics=sem, vmem_limit_bytes=VMEM_LIMIT)


def _sigmoid(x):
    return 0.5 * jnp.tanh(0.5 * x) + 0.5


def _softplus(x):
    return jnp.maximum(x, 0.0) + jnp.log1p(jnp.exp(-jnp.abs(x)))


def _inproj_kernel(x_ref, g_ref, w_ref, proj_ref, xn_ref):
    @pl.when(pl.program_id(1) == 0)
    def _():
        x = x_ref[...]
        ms = jnp.mean(x * x, axis=-1, keepdims=True)
        xn_ref[...] = (x * lax.rsqrt(ms + EPS) * g_ref[...]).astype(BF16)

    proj_ref[...] = jnp.dot(xn_ref[...], w_ref[...], preferred_element_type=F32)


def _inproj(x2d, g_all, w_all, layer, bm, bn):
    m, d = x2d.shape
    n = w_all.shape[2]
    return pl.pallas_call(
        _inproj_kernel,
        grid=(m // bm, n // bn),
        in_specs=[pl.BlockSpec((bm, d), lambda i, j: (i, 0)),
                  pl.BlockSpec((None, 1, d), lambda i, j: (layer, 0, 0)),
                  pl.BlockSpec((None, d, bn), lambda i, j: (layer, 0, j))],
        out_specs=pl.BlockSpec((bm, bn), lambda i, j: (i, j)),
        out_shape=jax.ShapeDtypeStruct((m, n), F32),
        scratch_shapes=[pltpu.VMEM((bm, d), BF16)],
        compiler_params=_cparams(("parallel", "arbitrary")),
        name="inproj",
    )(x2d, g_all, w_all)


def _hgrn_level_matrix(c):
    nlev = int(math.log2(c))
    assert 1 << nlev == c
    t = np.arange(c)[:, None]
    j = np.arange(c)[None, :]
    mats = []
    for lev in range(nlev):
        m = 2 << lev
        mid = (t // m) * m + m // 2 - 1
        second = (t % m) >= m // 2
        q_side = second & (j > mid) & (j <= t)
        k_side = (~second) & (j > t) & (j <= mid)
        mats.append(q_side | k_side)
    mats.append(j <= t)
    mats.append(j > t)
    return np.concatenate(mats, axis=0).astype(np.float32), nlev


def _hgrn_kernel(q_ref, f_ref, i_ref, g_ref, s0_ref, loglb_ref, log1mlb_ref, omlb_ref, gn_ref, m_ref,
                 o_ref, snew_ref, st_ref, *, c, heads, dk, n_valid, nlev, gs):
    ci = pl.program_id(1)

    @pl.when(ci == 0)
    def _():
        for g in range(gs):
            for h in range(heads):
                st_ref[g, h] = s0_ref[g, h].T

    row = lax.broadcasted_iota(jnp.int32, (c, 1), 0)
    rr = lax.broadcasted_iota(jnp.int32, (c, c), 0)
    cc = lax.broadcasted_iota(jnp.int32, (c, c), 1)
    rxc = rr ^ cc
    mstack = m_ref[...]

    hsl = [slice(h * dk, (h + 1) * dk) for h in range(heads)]
    for g in range(gs):
        zf = f_ref[g]
        e = jnp.exp(-jnp.abs(zf))
        r = 1.0 / (1.0 + e)
        logsig = jnp.minimum(zf, 0.0) - jnp.log1p(e)
        a = loglb_ref[...]
        b = log1mlb_ref[...] + logsig
        log_f = jnp.maximum(a, b) + jnp.log1p(jnp.exp(-jnp.abs(a - b)))
        k_in = omlb_ref[...] * jnp.where(zf >= 0, e * r, r)
        aq = q_ref[g]
        q = aq * _sigmoid(aq)
        v16 = i_ref[g].astype(BF16)
        if n_valid < c:
            valid = row < n_valid
            log_f = jnp.where(valid, log_f, 0.0)
            k_in = jnp.where(valid, k_in, 0.0)
        lf_hi = log_f.astype(BF16)
        lf_lo = (log_f - lf_hi.astype(F32)).astype(BF16)
        ex = jnp.exp(jnp.dot(mstack, lf_hi, preferred_element_type=F32)
                     + jnp.dot(mstack, lf_lo, preferred_element_type=F32))
        cum_x = ex[nlev * c:(nlev + 1) * c]
        qd = (q * cum_x).astype(BF16)
        kd = (k_in * ex[(nlev + 1) * c:(nlev + 2) * c]).astype(BF16)
        qk = q * k_in

        sts = [st_ref[g, h] for h in range(heads)]
        o_inter = [lax.dot_general(qd[:, hsl[h]], sts[h].astype(BF16), _NT, preferred_element_type=F32)
                   for h in range(heads)]
        upd = [lax.dot_general(v16[:, hsl[h]], kd[:, hsl[h]], _TN, preferred_element_type=F32)
               for h in range(heads)]
        als = []
        for lev in range(nlev):
            xl = ex[lev * c:(lev + 1) * c]
            second = (row & (1 << lev)) != 0
            ql = jnp.where(second, q * xl, 0.0).astype(BF16)
            kl = jnp.where(second, 0.0, k_in * xl).astype(BF16)
            als.append([lax.dot_general(ql[:, hsl[h]], kl[:, hsl[h]], _NT, preferred_element_type=F32)
                        for h in range(heads)])

        ag = g_ref[g]
        gate = gn_ref[...] * (ag * _sigmoid(ag))
        for h in range(heads):
            att = jnp.where(rxc == 0, jnp.sum(qk[:, hsl[h]], axis=-1, keepdims=True), 0.0)
            for lev in range(nlev):
                att = att + jnp.where(rxc < (2 << lev), als[lev][h], 0.0)
            o = o_inter[h] + jnp.dot(att.astype(BF16), v16[:, hsl[h]], preferred_element_type=F32)
            st_ref[g, h] = sts[h] * cum_x[c - 1:c, hsl[h]] + upd[h]
            ms = jnp.mean(o * o, axis=-1, keepdims=True)
            o_ref[g, :, hsl[h]] = (o * lax.rsqrt(ms + EPS) * gate[:, hsl[h]]).astype(o_ref.dtype)

    @pl.when(ci == pl.num_programs(1) - 1)
    def _():
        for g in range(gs):
            for h in range(heads):
                snew_ref[g, h] = st_ref[g, h].T


def _hgrn(proj3, s0, loglb, log1mlb, omlb, gn_all, layer, c, n_valid, gs, cols):
    bsz, lpad, _ = proj3.shape
    _, heads, dk, dv = s0.shape[-4:]
    assert dk == LANES and dv == LANES and bsz % gs == 0
    w = heads * dk
    mstack, nlev = _hgrn_level_matrix(c)
    mstack = jnp.asarray(mstack).astype(BF16)
    cq, cf, ci_, cg = (x // w for x in cols)
    kern = functools.partial(_hgrn_kernel, c=c, heads=heads, dk=dk, n_valid=n_valid, nlev=nlev, gs=gs)
    s0_lead = s0.ndim - 4

    def s0_map(b, t):
        return ((layer,) if s0_lead == 1 else ()) + (b, 0, 0, 0)

    s0_block = ((None,) if s0_lead == 1 else ()) + (gs, heads, dk, dv)

    def colspec(cb):
        return pl.BlockSpec((gs, c, w), lambda b, t: (b, t, cb))

    def pspec():
        return pl.BlockSpec((None, 1, w), lambda b, t: (layer, 0, 0))

    return pl.pallas_call(
        kern,
        grid=(bsz // gs, lpad // c),
        in_specs=[colspec(cq), colspec(cf), colspec(ci_), colspec(cg),
                  pl.BlockSpec(s0_block, s0_map),
                  pspec(), pspec(), pspec(), pspec(),
                  pl.BlockSpec(mstack.shape, lambda b, t: (0, 0))],
        out_specs=[pl.BlockSpec((gs, c, w), lambda b, t: (b, t, 0)),
                   pl.BlockSpec((gs, heads, dk, dv), lambda b, t: (b, 0, 0, 0))],
        out_shape=[jax.ShapeDtypeStruct((bsz, lpad, w), BF16),
                   jax.ShapeDtypeStruct((bsz, heads, dk, dv), F32)],
        scratch_shapes=[pltpu.VMEM((gs, heads, dv, dk), F32)],
        compiler_params=_cparams(("parallel", "arbitrary")),
        name="hgrn",
    )(proj3, proj3, proj3, proj3, s0, loglb, log1mlb, omlb, gn_all, mstack)


def _lru_kernel(x_ref, g_ref, buf0_ref, h0_ref, cw_ref, cb_ref, wr_ref, br_ref, wi_ref, bi_ref, lam_ref,
                o_ref, hfin_ref, bufnew_ref, xp_ref, a_ref, u_ref, h_ref, *, tl, n_last, conv_w, nblk, bs):
    ti = pl.program_id(1)
    nt = pl.num_programs(1)
    pad = SUBLANES
    tail = conv_w - 1

    @pl.when(ti == 0)
    def _():
        xp_ref[0:pad, :] = jnp.zeros((pad, xp_ref.shape[1]), F32)
        xp_ref[pad - tail:pad, :] = buf0_ref[...]
        h_ref[...] = jnp.broadcast_to(h0_ref[...], h_ref.shape)

    xp_ref[pad:pad + tl, :] = x_ref[...]
    xc = cb_ref[...]
    for j in range(conv_w):
        xc = xc + xp_ref[pad - tail + j:pad - tail + j + tl, :] * cw_ref[j:j + 1, :]

    sp_lam = _softplus(-lam_ref[...])
    for n in range(nblk):
        sl = slice(n * bs, (n + 1) * bs)
        xb = xc[:, sl]
        xb16 = xb.astype(BF16)
        r = _sigmoid(jnp.dot(xb16, wr_ref[n], preferred_element_type=F32) + br_ref[:, sl])
        i = _sigmoid(jnp.dot(xb16, wi_ref[n], preferred_element_type=F32) + bi_ref[:, sl])
        log_a = -LRU_C * r * sp_lam[:, sl]
        a = jnp.exp(log_a)
        a_ref[0:tl, sl] = a
        u_ref[0:tl, sl] = jnp.sqrt(-jnp.tanh(log_a) * (a * a + 1.0)) * (i * xb)

    srow = lax.broadcasted_iota(jnp.int32, (SUBLANES, 1), 0)

    def tile_scan(base, hprev):
        a = a_ref[pl.ds(base, SUBLANES), :]
        u = u_ref[pl.ds(base, SUBLANES), :]
        for d in (1, 2, 4):
            a_s = jnp.where(srow >= d, pltpu.roll(a, d, 0), 1.0)
            u_s = jnp.where(srow >= d, pltpu.roll(u, d, 0), 0.0)
            u = u + a * u_s
            a = a * a_s
        hs = u + a * hprev
        a_ref[pl.ds(base, SUBLANES), :] = hs
        return hs

    n8 = -(-tl // SUBLANES)

    def body(k, hprev):
        base = pl.multiple_of(k * SUBLANES, SUBLANES)
        hs = tile_scan(base, hprev)
        return jnp.broadcast_to(hs[SUBLANES - 1:SUBLANES, :], hs.shape)

    if n8 > 1:
        hlast = lax.fori_loop(0, n8 - 1, body, h_ref[...])
    else:
        hlast = h_ref[...]
    hs_last = tile_scan((n8 - 1) * SUBLANES, hlast)
    lr_full = (tl - 1) % SUBLANES
    h_ref[...] = jnp.broadcast_to(hs_last[lr_full:lr_full + 1, :], h_ref.shape)

    gg = g_ref[...]
    o_ref[...] = (a_ref[0:tl, :] * (gg * _sigmoid(gg))).astype(o_ref.dtype)

    @pl.when(ti == nt - 1)
    def _():
        lr = (n_last - 1) % SUBLANES
        hfin_ref[...] = hs_last[lr:lr + 1, :]
        bufnew_ref[...] = xp_ref[pad + n_last - tail:pad + n_last, :]

    @pl.when(ti < nt - 1)
    def _():
        keep = xp_ref[tl:tl + pad, :]
        xp_ref[0:pad, :] = keep


def _lru(proj3, buf0, h0, cw_all, cb_all, wr_all, br_all, wi_all, bi_all, lam_all, layer, tl, n_last, cols):
    bsz, lpad, _ = proj3.shape
    conv_w, w = cw_all.shape[1:]
    nblk, bs = wr_all.shape[1:3]
    cx, cg = (x // w for x in cols)
    nt = lpad // tl
    assert n_last == tl or nt == 1
    tlp = -(-tl // SUBLANES) * SUBLANES
    kern = functools.partial(_lru_kernel, tl=tl, n_last=n_last, conv_w=conv_w, nblk=nblk, bs=bs)
    lead = buf0.ndim - 3

    def st_map(b, t):
        return ((layer,) if lead == 1 else ()) + (b, 0, 0)

    def st_block(rows):
        return ((None,) if lead == 1 else ()) + (None, rows, w)

    def pspec(rows):
        return pl.BlockSpec((None, rows, w), lambda b, t: (layer, 0, 0))

    def wspec():
        return pl.BlockSpec((None, nblk, bs, bs), lambda b, t: (layer, 0, 0, 0))

    return pl.pallas_call(
        kern,
        grid=(bsz, nt),
        in_specs=[pl.BlockSpec((None, tl, w), lambda b, t: (b, t, cx)),
                  pl.BlockSpec((None, tl, w), lambda b, t: (b, t, cg)),
                  pl.BlockSpec(st_block(conv_w - 1), st_map),
                  pl.BlockSpec(st_block(1), st_map),
                  pspec(conv_w), pspec(1), wspec(), pspec(1), wspec(), pspec(1), pspec(1)],
        out_specs=[pl.BlockSpec((None, tl, w), lambda b, t: (b, t, 0)),
                   pl.BlockSpec((None, 1, w), lambda b, t: (b, 0, 0)),
                   pl.BlockSpec((None, conv_w - 1, w), lambda b, t: (b, 0, 0))],
        out_shape=[jax.ShapeDtypeStruct((bsz, lpad, w), BF16),
                   jax.ShapeDtypeStruct((bsz, 1, w), F32),
                   jax.ShapeDtypeStruct((bsz, conv_w - 1, w), F32)],
        scratch_shapes=[pltpu.VMEM((tlp + 2 * SUBLANES, w), F32),
                        pltpu.VMEM((tlp, w), F32),
                        pltpu.VMEM((tlp, w), F32),
                        pltpu.VMEM((SUBLANES, w), F32)],
        compiler_params=_cparams(("parallel", "arbitrary")),
        name="lru",
    )(proj3, proj3, buf0, h0, cw_all, cb_all, wr_all, br_all, wi_all, bi_all, lam_all)


def _tri_ext(tk):
    s1 = np.arange(tk)[:, None]
    s0 = np.arange(tk)[None, :]
    return np.concatenate([(s1 > s0), np.ones((tk, tk), bool)], axis=1).astype(np.float32)


def _sb_terms(z2):
    mx = jnp.maximum(z2, 0.0)
    mn = jnp.minimum(z2, 0.0)
    t = jnp.log2(1.0 + jnp.exp2(mn - mx))
    return t + mx, mn - t


def _sbp_kernel(bias_ref, q_ref, k_ref, v_ref, g_ref, tri_ref, o_ref, k16_ref, v16_ref, *, tq, dh, nq, group, unroll):
    h = pl.program_id(1)
    qscale = dh ** -0.5 * LOG2E
    bias2 = bias_ref[h] * LOG2E
    tri = tri_ref[...]
    for j in range(nq):
        p = nq - 1 - j
        k16_ref[p * tq:(p + 1) * tq, :] = k_ref[j * tq:(j + 1) * tq, :].astype(BF16)
        v16_ref[p * tq:(p + 1) * tq, :] = v_ref[j * tq:(j + 1) * tq, :].astype(BF16)
    cmr = lax.broadcasted_iota(jnp.int32, (tq, tq), 1) - lax.broadcasted_iota(jnp.int32, (tq, tq), 0)

    def q_block(qi, first, nkb):
        r0 = pl.multiple_of(qi * tq, tq)
        lo = (nq - nkb) * tq
        q = (q_ref[pl.ds(r0, tq), :] * qscale).astype(BF16)
        z2 = lax.dot_general(q, k16_ref[lo:nq * tq, :], _NT, preferred_element_type=F32) + bias2
        sp, ls = _sb_terms(z2)
        masks = {}
        sp_blocks = []
        for p in range(nkb):
            j = nkb - 1 - p
            blk = sp[:, p * tq:(p + 1) * tq]
            if j >= first:
                masks[p] = cmr < (qi - j) * tq
                blk = jnp.where(masks[p], blk, 0.0)
            sp_blocks.append(blk.astype(BF16))
        ext = jnp.dot(jnp.concatenate(sp_blocks, axis=0), tri, preferred_element_type=F32)
        run = jnp.zeros((tq, tq), F32)
        w_blocks = []
        for p in range(nkb):
            e = ext[p * tq:(p + 1) * tq]
            wp = jnp.exp2(ls[:, p * tq:(p + 1) * tq] - e[:, :tq] - run)
            if p in masks:
                wp = jnp.where(masks[p], wp, 0.0)
            w_blocks.append(wp.astype(BF16))
            run = run + e[:, tq:]
        acc = jnp.dot(jnp.concatenate(w_blocks, axis=1), v16_ref[lo:nq * tq, :], preferred_element_type=F32)
        gg = g_ref[pl.ds(r0, tq), :]
        o_ref[pl.ds(r0, tq), :] = (acc * (gg * _sigmoid(gg))).astype(o_ref.dtype)

    for first in range(0, nq, group):
        nkb = first + group

        def body(i, carry, first=first, nkb=nkb):
            for u in range(unroll):
                q_block(first + i * unroll + u, first, nkb)
            return carry

        lax.fori_loop(0, group // unroll, body, 0)


def _sb_prompt(proj3, bias_all, layer, heads, dh, tq, group, unroll, cols):
    bsz, seqlen, _ = proj3.shape
    nq = seqlen // tq
    assert nq % group == 0 and group % unroll == 0
    cq, ck, cv, cg = (x // dh for x in cols)
    tri = jnp.asarray(_tri_ext(tq)).astype(BF16)
    kern = functools.partial(_sbp_kernel, tq=tq, dh=dh, nq=nq, group=group, unroll=unroll)

    def colspec(c0):
        return pl.BlockSpec((None, seqlen, dh), lambda b, h: (b, 0, c0 + h))

    return pl.pallas_call(
        kern,
        grid=(bsz, heads),
        in_specs=[pl.BlockSpec(memory_space=pltpu.SMEM),
                  colspec(cq), colspec(ck), colspec(cv), colspec(cg),
                  pl.BlockSpec(tri.shape, lambda b, h: (0, 0))],
        out_specs=pl.BlockSpec((None, seqlen, dh), lambda b, h: (b, 0, h)),
        out_shape=jax.ShapeDtypeStruct((bsz, seqlen, heads * dh), BF16),
        scratch_shapes=[pltpu.VMEM((seqlen, dh), BF16), pltpu.VMEM((seqlen, dh), BF16)],
        compiler_params=_cparams(("parallel", "parallel")),
        name="sb_prompt",
    )(bias_all[layer], proj3, proj3, proj3, proj3, tri)


def _sbs_kernel(pt_ref, bias_ref, q_ref, kc_ref, vc_ref, g_ref, *rest, heads, dh, n_valid, page, pps):
    kp_refs, vp_refs = rest[:pps], rest[pps:2 * pps]
    tri_ref, o_ref, qbd_ref, k16_ref, v16_ref, acc_ref, run_ref = rest[2 * pps:]
    j = pl.program_id(1)
    nr = SUBLANES
    hr = heads * nr
    w = heads * dh
    tri = tri_ref[...]
    bias2 = jnp.concatenate([jnp.full((nr, page), bias_ref[h] * LOG2E, F32) for h in range(heads)], axis=0)

    def sweep(qbd, k16, v16, mask, acc, run):
        nb = k16.shape[0] // page
        z2 = lax.dot_general(qbd, k16, _NT, preferred_element_type=F32)
        sps, lss = [], []
        for p in range(nb):
            sp, ls = _sb_terms(z2[:, p * page:(p + 1) * page] + bias2)
            if mask is not None:
                sp = jnp.where(mask, sp, 0.0)
            sps.append(sp.astype(BF16))
            lss.append(ls)
        ext = jnp.dot(jnp.concatenate(sps, axis=0), tri, preferred_element_type=F32)
        ws = []
        for p in range(nb):
            e = ext[p * hr:(p + 1) * hr]
            wp = jnp.exp2(lss[p] - e[:, :page] - run)
            if mask is not None:
                wp = jnp.where(mask, wp, 0.0)
            ws.append(wp.astype(BF16))
            run = run + e[:, page:]
        return acc + jnp.dot(jnp.concatenate(ws, axis=1), v16, preferred_element_type=F32), run

    @pl.when(j == 0)
    def _():
        rh = lax.broadcasted_iota(jnp.int32, (hr, w), 0) // nr
        ch = lax.broadcasted_iota(jnp.int32, (hr, w), 1) // dh
        qt = jnp.concatenate([q_ref[...] * (dh ** -0.5 * LOG2E)] * heads, axis=0)
        qbd_ref[...] = jnp.where(rh == ch, qt, 0.0).astype(BF16)
        zpad = jnp.zeros((page - nr, w), F32)
        colp = lax.broadcasted_iota(jnp.int32, (hr, page), 1)
        trow = lax.broadcasted_iota(jnp.int32, (hr, page), 0) & (nr - 1)
        kcur = jnp.concatenate([kc_ref[...], zpad], axis=0).astype(BF16)
        vcur = jnp.concatenate([vc_ref[...], zpad], axis=0).astype(BF16)
        mask = (colp < trow) & (colp < n_valid)
        acc, run = sweep(qbd_ref[...], kcur, vcur, mask, jnp.zeros((hr, w), F32), jnp.zeros((hr, page), F32))
        acc_ref[...] = acc
        run_ref[...] = run

    for p in range(pps):
        for h in range(heads):
            rows, cols = slice(p * page, (p + 1) * page), slice(h * dh, (h + 1) * dh)
            k16_ref[rows, cols] = kp_refs[p][pl.ds(h, page, stride=heads), :].astype(BF16)
            v16_ref[rows, cols] = vp_refs[p][pl.ds(h, page, stride=heads), :].astype(BF16)
    acc, run = sweep(qbd_ref[...], k16_ref[...], v16_ref[...], None, acc_ref[...], run_ref[...])
    acc_ref[...] = acc
    run_ref[...] = run

    @pl.when(j == pl.num_programs(1) - 1)
    def _():
        for h in range(heads):
            sl = slice(h * dh, (h + 1) * dh)
            gg = g_ref[:, sl]
            o_ref[:, sl] = (acc[h * nr:(h + 1) * nr, sl] * (gg * _sigmoid(gg))).astype(o_ref.dtype)


def _sb_sample(proj3, cache_k, cache_v, page_table, bias_all, layer, n_valid, pps, cols):
    bsz, lpad, _ = proj3.shape
    assert lpad == SUBLANES
    depth, n_phys, page, heads, dh = cache_k.shape
    n_pages = page_table.shape[1]
    assert n_pages % pps == 0
    w = heads * dh
    cq, ck, cv, cg = (x // w for x in cols)
    kp = cache_k.reshape(depth, n_phys, page * heads, dh)
    vp = cache_v.reshape(depth, n_phys, page * heads, dh)
    tri = jnp.asarray(_tri_ext(page)).astype(BF16)
    kern = functools.partial(_sbs_kernel, heads=heads, dh=dh, n_valid=n_valid, page=page, pps=pps)

    def colspec(cb):
        return pl.BlockSpec((None, lpad, w), lambda b, j, pt: (b, 0, cb))

    def pagespec(p):
        return pl.BlockSpec((None, None, page * heads, dh),
                            lambda b, j, pt: (layer, pt[b, n_pages - 1 - (j * pps + p)], 0, 0))

    grid_spec = pltpu.PrefetchScalarGridSpec(
        num_scalar_prefetch=1,
        grid=(bsz, n_pages // pps),
        in_specs=[pl.BlockSpec(memory_space=pltpu.SMEM),
                  colspec(cq), colspec(ck), colspec(cv), colspec(cg)]
                 + [pagespec(p) for p in range(pps)] + [pagespec(p) for p in range(pps)]
                 + [pl.BlockSpec(tri.shape, lambda b, j, pt: (0, 0))],
        out_specs=pl.BlockSpec((None, lpad, w), lambda b, j, pt: (b, 0, 0)),
        scratch_shapes=[pltpu.VMEM((heads * SUBLANES, w), BF16),
                        pltpu.VMEM((pps * page, w), BF16), pltpu.VMEM((pps * page, w), BF16),
                        pltpu.VMEM((heads * SUBLANES, w), F32), pltpu.VMEM((heads * SUBLANES, page), F32)],
    )
    return pl.pallas_call(
        kern,
        grid_spec=grid_spec,
        out_shape=jax.ShapeDtypeStruct((bsz, lpad, w), BF16),
        compiler_params=_cparams(("parallel", "arbitrary")),
        name="sb_sample",
    )(page_table, bias_all[layer], proj3, proj3, proj3, proj3, *([kp] * pps), *([vp] * pps), tri)


def _merge_kernel(oa_ref, ob_ref, oc_ref, ga_ref, gb_ref, gc_ref, x_ref, wa_ref, wb_ref, wc_ref, wo_ref, gp_ref,
                  y_ref):
    merged = (_sigmoid(ga_ref[...]) * jnp.dot(oa_ref[...], wa_ref[...], preferred_element_type=F32)
              + _sigmoid(gb_ref[...]) * jnp.dot(ob_ref[...], wb_ref[...], preferred_element_type=F32)
              + _sigmoid(gc_ref[...]) * jnp.dot(oc_ref[...], wc_ref[...], preferred_element_type=F32))
    out = jnp.dot(merged.astype(BF16), wo_ref[...], preferred_element_type=F32)
    ms = jnp.mean(out * out, axis=-1, keepdims=True)
    y_ref[...] = x_ref[...] + out * lax.rsqrt(ms + EPS) * gp_ref[...]


def _merge(oa, ob, oc, proj, gate_col, x2d, wa_all, wb_all, wc_all, wo_all, gp_all, layer, bm):
    m, d = x2d.shape
    wbr = oa.shape[1]
    assert gate_col % d == 0

    def ospec():
        return pl.BlockSpec((bm, wbr), lambda i: (i, 0))

    def gspec(k):
        return pl.BlockSpec((bm, d), lambda i: (i, gate_col // d + k))

    def wspec(rows):
        return pl.BlockSpec((None, rows, d), lambda i: (layer, 0, 0), pipeline_mode=pl.Buffered(1))

    return pl.pallas_call(
        _merge_kernel,
        grid=(m // bm,),
        in_specs=[ospec(), ospec(), ospec(), gspec(0), gspec(1), gspec(2),
                  pl.BlockSpec((bm, d), lambda i: (i, 0)),
                  wspec(wbr), wspec(wbr), wspec(wbr), wspec(d),
                  pl.BlockSpec((None, 1, d), lambda i: (layer, 0, 0))],
        out_specs=pl.BlockSpec((bm, d), lambda i: (i, 0)),
        out_shape=jax.ShapeDtypeStruct((m, d), F32),
        compiler_params=_cparams(("parallel",)),
        name="merge",
    )(oa, ob, oc, proj, proj, proj, x2d, wa_all, wb_all, wc_all, wo_all, gp_all)


def _kvout_kernel(*refs, depth, heads, dh, tm):
    k_ins, v_ins = refs[:depth], refs[depth:2 * depth]
    ko_ref, vo_ref = refs[2 * depth:]
    layer = pl.program_id(0)
    for l in range(depth):
        @pl.when(layer == l)
        def _(l=l):
            for h in range(heads):
                ko_ref[pl.ds(h, tm, stride=heads), :] = k_ins[l][:, h * dh:(h + 1) * dh]
                vo_ref[pl.ds(h, tm, stride=heads), :] = v_ins[l][:, h * dh:(h + 1) * dh]


def _kvout(projs, heads, dh, ck, cv, tm):
    depth = len(projs)
    m = projs[0].shape[0]
    w = heads * dh
    nt = m // tm

    def inspec(l, col):
        return pl.BlockSpec((tm, w), lambda d_, i: (jnp.where(d_ == l, i, jnp.where(d_ < l, 0, nt - 1)), col // w))

    out_spec = pl.BlockSpec((None, tm * heads, dh), lambda d_, i: (d_, i, 0))
    out_sds = jax.ShapeDtypeStruct((depth, m * heads, dh), F32)
    return pl.pallas_call(
        functools.partial(_kvout_kernel, depth=depth, heads=heads, dh=dh, tm=tm),
        grid=(depth, nt),
        in_specs=[inspec(l, ck) for l in range(depth)] + [inspec(l, cv) for l in range(depth)],
        out_specs=[out_spec, out_spec],
        out_shape=[out_sds, out_sds],
        compiler_params=_cparams(("arbitrary", "arbitrary")),
        name="kvout",
    )(*projs, *projs)


def _pick(n, prefs):
    for p in prefs:
        if n % p == 0:
            return p
    return n


def kernel(x_prompt, x_sample, cache_k, cache_v, state_hgrn, state_lru_h, state_conv, page_table, norm_pre, norm_post, w_in, hgrn_lb_logits, hgrn_norm, lru_conv_w, lru_conv_b, lru_w_r, lru_b_r, lru_w_i, lru_b_i, lru_lambda, sb_bias, w_branch_a, w_branch_b, w_branch_c, w_out):
    depth, d, n_cols = w_in.shape
    pb, seq, _ = x_prompt.shape
    sb, dec_seq, _ = x_sample.shape
    _, _, a_heads, a_dk, a_dv = state_hgrn.shape
    a_kw, a_w = a_heads * a_dk, a_heads * a_dv
    b_w = state_lru_h.shape[2]
    conv_w = lru_conv_w.shape[1]
    _, _, page, c_heads, c_dh = cache_k.shape
    c_w = c_heads * c_dh
    sizes = (a_kw, a_kw, a_w, a_w, b_w, b_w, c_w, c_w, c_w, c_w, d, d, d)
    assert sum(sizes) == n_cols and a_kw == a_w
    offs = np.concatenate([[0], np.cumsum(sizes)]).tolist()
    (o_aq, o_af, o_ai, o_ag, o_bx, o_bg, o_cq, o_ck, o_cv, o_cg, o_ma) = offs[:11]

    lbs = jnp.cumsum(jax.nn.softmax(hgrn_lb_logits.astype(F32), axis=0), axis=0)
    lbs = lbs - lbs[0:1]
    loglb = jnp.log(lbs)[:, None, :]
    log1mlb = jnp.log1p(-lbs)[:, None, :]
    omlb = (1.0 - lbs)[:, None, :]
    r3 = lambda p: p.astype(F32)[:, None, :]
    norm_pre3, norm_post3, hgrn_norm3 = r3(norm_pre), r3(norm_post), r3(hgrn_norm)
    conv_b3, b_r3, b_i3, lam3 = r3(lru_conv_b), r3(lru_b_r), r3(lru_b_i), r3(lru_lambda)
    conv_w3 = lru_conv_w.astype(F32)
    w_in16 = w_in.astype(BF16)
    w_r16, w_i16 = lru_w_r.astype(BF16), lru_w_i.astype(BF16)
    w_a16, w_b16, w_c16, w_o16 = (t.astype(BF16) for t in (w_branch_a, w_branch_b, w_branch_c, w_out))
    sbias = sb_bias.astype(F32)

    lpad = SUBLANES
    assert dec_seq <= lpad and dec_seq >= conv_w - 1
    xs_pad = jnp.pad(x_sample, ((0, 0), (0, lpad - dec_seq), (0, 0)))

    mp, ms = pb * seq, sb * lpad
    bn = _pick(n_cols, (1024, 512, 256, 128))
    hc = _pick(seq, (128, 64, 32, 16, 8))
    tl = _pick(seq, (256, 128, 64, 32, 16, 8))
    tq = _pick(seq, (128,))
    sbg = _pick(seq // tq, (2, 1))
    sbu = sbg
    hgs = _pick(sb, (4, 2, 1))
    pps = _pick(page_table.shape[1], (8, 4, 2, 1))
    hp, hs = x_prompt.reshape(mp, d), xs_pad.reshape(ms, d)
    zs_p = jnp.zeros((pb, a_heads, a_dk, a_dv), F32)
    zh_p = jnp.zeros((pb, 1, b_w), F32)
    zb_p = jnp.zeros((pb, conv_w - 1, b_w), F32)
    h0_s = state_lru_h[:, :, None, :]

    outs = [[] for _ in range(6)]
    projs = ([], [])
    for l in range(depth):
        for grp in range(2):
            if grp == 0:
                x2d, bsz, lp, nv = hp, pb, seq, seq
            else:
                x2d, bsz, lp, nv = hs, sb, lpad, dec_seq
            m = x2d.shape[0]
            bm = _pick(m, (1024, 512, 256, 128, 64, 32, 16, 8))
            proj = _inproj(x2d, norm_pre3, w_in16, l, bm, bn)
            main3 = proj.reshape(bsz, lp, n_cols)
            if grp == 0:
                o_a, s_new = _hgrn(main3, zs_p, loglb, log1mlb, omlb, hgrn_norm3, l, hc, hc, 1,
                                   (o_aq, o_af, o_ai, o_ag))
                o_b, h_new, buf_new = _lru(main3, zb_p, zh_p, conv_w3, conv_b3, w_r16, b_r3, w_i16, b_i3, lam3, l,
                                           tl, tl, (o_bx, o_bg))
                o_c = _sb_prompt(main3, sbias, l, c_heads, c_dh, tq, sbg, sbu, (o_cq, o_ck, o_cv, o_cg))
            else:
                o_a, s_new = _hgrn(main3, state_hgrn, loglb, log1mlb, omlb, hgrn_norm3, l, lpad, nv, hgs,
                                   (o_aq, o_af, o_ai, o_ag))
                o_b, h_new, buf_new = _lru(main3, state_conv, h0_s, conv_w3, conv_b3, w_r16, b_r3, w_i16, b_i3,
                                           lam3, l, lpad, nv, (o_bx, o_bg))
                o_c = _sb_sample(main3, cache_k, cache_v, page_table, sbias, l, nv, pps, (o_cq, o_ck, o_cv, o_cg))
            bmm = _pick(m, (256, 128, 64, 32, 16, 8))
            y = _merge(o_a.reshape(m, a_w), o_b.reshape(m, b_w), o_c.reshape(m, c_w), proj, o_ma, x2d,
                       w_a16, w_b16, w_c16, w_o16, norm_post3, l, bmm)
            projs[grp].append(proj)
            for lst, val in zip(outs[grp * 3:(grp + 1) * 3], (s_new, h_new[:, 0, :], buf_new)):
                lst.append(val)
            if grp == 0:
                hp = y
            else:
                hs = y
    y_p = hp.reshape(pb, seq, d)
    y_s = hs.reshape(sb, lpad, d)[:, :dec_seq]
    pk, pv = (t.reshape(depth, pb, seq, c_heads, c_dh)
              for t in _kvout(projs[0], c_heads, c_dh, o_ck, o_cv, _pick(mp, (512, 256, 128, 64, 32, 16, 8))))
    sk, sv = (t.reshape(depth, sb, lpad, c_heads, c_dh)[:, :, :dec_seq]
              for t in _kvout(projs[1], c_heads, c_dh, o_ck, o_cv, _pick(ms, (512, 256, 128, 64, 32, 16, 8))))
    ps, ph, pc, ss, sh, sc = (jnp.stack(o) for o in outs)
    return (y_p, y_s, pk, pv, ps, ph, pc, sk, sv, ss, sh, sc)
```

```python
import functools
import math

import numpy as np
import jax
import jax.numpy as jnp
from jax import lax
from jax.experimental import pallas as pl
from jax.experimental.pallas import tpu as pltpu

F32 = jnp.float32
BF16 = jnp.bfloat16
EPS = 1e-6
LRU_C = 8.0
LOG2E = 1.4426950408889634
LANES = 128
SUBLANES = 8
VMEM_LIMIT = 56 * 1024 * 1024

_NT = (((1,), (1,)), ((), ()))
_TN = (((0,), (0,)), ((), ()))


def _cparams(sem):
    return pltpu.CompilerParams(dimension_semantics=sem, vmem_limit_bytes=VMEM_LIMIT)


def _sigmoid(x):
    return 0.5 * jnp.tanh(0.5 * x) + 0.5


def _softplus(x):
    return jnp.maximum(x, 0.0) + jnp.log1p(jnp.exp(-jnp.abs(x)))


def _inproj_kernel(x_ref, g_ref, w_ref, proj_ref, xn_ref):
    @pl.when(pl.program_id(1) == 0)
    def _():
        x = x_ref[...]
        ms = jnp.mean(x * x, axis=-1, keepdims=True)
        xn_ref[...] = (x * lax.rsqrt(ms + EPS) * g_ref[...]).astype(BF16)

    proj_ref[...] = jnp.dot(xn_ref[...], w_ref[...], preferred_element_type=F32)


def _inproj(x2d, g_all, w_all, layer, bm, bn):
    m, d = x2d.shape
    n = w_all.shape[2]
    return pl.pallas_call(
        _inproj_kernel,
        grid=(m // bm, n // bn),
        in_specs=[pl.BlockSpec((bm, d), lambda i, j: (i, 0)),
                  pl.BlockSpec((None, 1, d), lambda i, j: (layer, 0, 0)),
                  pl.BlockSpec((None, d, bn), lambda i, j: (layer, 0, j))],
        out_specs=pl.BlockSpec((bm, bn), lambda i, j: (i, j)),
        out_shape=jax.ShapeDtypeStruct((m, n), F32),
        scratch_shapes=[pltpu.VMEM((bm, d), BF16)],
        compiler_params=_cparams(("parallel", "arbitrary")),
        name="inproj",
    )(x2d, g_all, w_all)


def _hgrn_level_matrix(c):
    nlev = int(math.log2(c))
    assert 1 << nlev == c
    t = np.arange(c)[:, None]
    j = np.arange(c)[None, :]
    mats = []
    for lev in range(nlev):
        m = 2 << lev
        mid = (t // m) * m + m // 2 - 1
        second = (t % m) >= m // 2
        q_side = second & (j > mid) & (j <= t)
        k_side = (~second) & (j > t) & (j <= mid)
        mats.append(q_side | k_side)
    mats.append(j <= t)
    mats.append(j > t)
    return np.concatenate(mats, axis=0).astype(np.float32), nlev


def _hgrn_kernel(q_ref, f_ref, i_ref, g_ref, s0_ref, loglb_ref, log1mlb_ref, omlb_ref, gn_ref, m_ref,
                 o_ref, snew_ref, st_ref, *, c, heads, dk, n_valid, nlev, gs):
    ci = pl.program_id(1)

    @pl.when(ci == 0)
    def _():
        for g in range(gs):
            for h in range(heads):
                st_ref[g, h] = s0_ref[g, h].T

    row = lax.broadcasted_iota(jnp.int32, (c, 1), 0)
    rr = lax.broadcasted_iota(jnp.int32, (c, c), 0)
    cc = lax.broadcasted_iota(jnp.int32, (c, c), 1)
    rxc = rr ^ cc
    pair_lev = jnp.full((c, c), -1, jnp.int32)
    for lev in range(nlev):
        pair_lev = jnp.where((rr > cc) & (rxc >= (1 << lev)) & (rxc < (2 << lev)), lev, pair_lev)
    mstack = m_ref[...]

    hsl = [slice(h * dk, (h + 1) * dk) for h in range(heads)]
    for g in range(gs):
        zf = f_ref[g]
        e = jnp.exp(-jnp.abs(zf))
        r = 1.0 / (1.0 + e)
        logsig = jnp.minimum(zf, 0.0) - jnp.log1p(e)
        a = loglb_ref[...]
        b = log1mlb_ref[...] + logsig
        log_f = jnp.maximum(a, b) + jnp.log1p(jnp.exp(-jnp.abs(a - b)))
        k_in = omlb_ref[...] * jnp.where(zf >= 0, e * r, r)
        aq = q_ref[g]
        q = aq * _sigmoid(aq)
        v16 = i_ref[g].astype(BF16)
        if n_valid < c:
            valid = row < n_valid
            log_f = jnp.where(valid, log_f, 0.0)
            k_in = jnp.where(valid, k_in, 0.0)
        lf_hi = log_f.astype(BF16).astype(F32)
        lf2 = jnp.concatenate([lf_hi, log_f - lf_hi], axis=0).astype(BF16)
        ex = jnp.exp(jnp.dot(mstack, lf2, preferred_element_type=F32))
        cum_x = ex[nlev * c:(nlev + 1) * c]
        qd = (q * cum_x).astype(BF16)
        kd = (k_in * ex[(nlev + 1) * c:(nlev + 2) * c]).astype(BF16)
        qk = q * k_in

        sts = [st_ref[g, h] for h in range(heads)]
        o_inter = [lax.dot_general(qd[:, hsl[h]], sts[h].astype(BF16), _NT, preferred_element_type=F32)
                   for h in range(heads)]
        upd = [lax.dot_general(v16[:, hsl[h]], kd[:, hsl[h]], _TN, preferred_element_type=F32)
               for h in range(heads)]
        als = []
        for lev in range(nlev):
            second = (row & (1 << lev)) != 0
            y = (jnp.where(second, q, k_in) * ex[lev * c:(lev + 1) * c]).astype(BF16)
            als.append([lax.dot_general(y[:, hsl[h]], y[:, hsl[h]], _NT, preferred_element_type=F32)
                        for h in range(heads)])

        ag = g_ref[g]
        gate = gn_ref[...] * (ag * _sigmoid(ag))
        for h in range(heads):
            att = jnp.where(rxc == 0, jnp.sum(qk[:, hsl[h]], axis=-1, keepdims=True), 0.0)
            for lev in range(nlev):
                att = jnp.where(pair_lev == lev, als[lev][h], att)
            o = o_inter[h] + jnp.dot(att.astype(BF16), v16[:, hsl[h]], preferred_element_type=F32)
            st_ref[g, h] = sts[h] * cum_x[c - 1:c, hsl[h]] + upd[h]
            ms = jnp.mean(o * o, axis=-1, keepdims=True)
            o_ref[g, :, hsl[h]] = (o * lax.rsqrt(ms + EPS) * gate[:, hsl[h]]).astype(o_ref.dtype)

    @pl.when(ci == pl.num_programs(1) - 1)
    def _():
        for g in range(gs):
            for h in range(heads):
                snew_ref[g, h] = st_ref[g, h].T


def _hgrn(proj3, s0, loglb, log1mlb, omlb, gn_all, layer, c, n_valid, gs, cols):
    bsz, lpad, _ = proj3.shape
    _, heads, dk, dv = s0.shape[-4:]
    assert dk == LANES and dv == LANES and bsz % gs == 0
    w = heads * dk
    mstack, nlev = _hgrn_level_matrix(c)
    mstack = jnp.asarray(np.concatenate([mstack, mstack], axis=1)).astype(BF16)
    cq, cf, ci_, cg = (x // w for x in cols)
    kern = functools.partial(_hgrn_kernel, c=c, heads=heads, dk=dk, n_valid=n_valid, nlev=nlev, gs=gs)
    s0_lead = s0.ndim - 4

    def s0_map(b, t):
        return ((layer,) if s0_lead == 1 else ()) + (b, 0, 0, 0)

    s0_block = ((None,) if s0_lead == 1 else ()) + (gs, heads, dk, dv)

    def colspec(cb):
        return pl.BlockSpec((gs, c, w), lambda b, t: (b, t, cb))

    def pspec():
        return pl.BlockSpec((None, 1, w), lambda b, t: (layer, 0, 0))

    return pl.pallas_call(
        kern,
        grid=(bsz // gs, lpad // c),
        in_specs=[colspec(cq), colspec(cf), colspec(ci_), colspec(cg),
                  pl.BlockSpec(s0_block, s0_map),
                  pspec(), pspec(), pspec(), pspec(),
                  pl.BlockSpec(mstack.shape, lambda b, t: (0, 0))],
        out_specs=[pl.BlockSpec((gs, c, w), lambda b, t: (b, t, 0)),
                   pl.BlockSpec((gs, heads, dk, dv), lambda b, t: (b, 0, 0, 0))],
        out_shape=[jax.ShapeDtypeStruct((bsz, lpad, w), BF16),
                   jax.ShapeDtypeStruct((bsz, heads, dk, dv), F32)],
        scratch_shapes=[pltpu.VMEM((gs, heads, dv, dk), F32)],
        compiler_params=_cparams(("parallel", "arbitrary")),
        name="hgrn",
    )(proj3, proj3, proj3, proj3, s0, loglb, log1mlb, omlb, gn_all, mstack)


def _lru_kernel(*refs, gs, **kw):
    seq_in, params, seq_rest = refs[:4], refs[4:11], refs[11:]
    for g in range(gs):
        _lru_seq(*[r.at[g] for r in seq_in], *params, *[r.at[g] for r in seq_rest], **kw)


def _lru_seq(x_ref, g_ref, buf0_ref, h0_ref, cw_ref, cb_ref, wr_ref, br_ref, wi_ref, bi_ref, lam_ref,
             o_ref, hfin_ref, bufnew_ref, xp_ref, a_ref, u_ref, h_ref, *, tl, n_last, conv_w, nblk, bs):
    ti = pl.program_id(1)
    nt = pl.num_programs(1)
    pad = SUBLANES
    tail = conv_w - 1

    @pl.when(ti == 0)
    def _():
        xp_ref[0:pad, :] = jnp.zeros((pad, xp_ref.shape[1]), F32)
        xp_ref[pad - tail:pad, :] = buf0_ref[...]
        h_ref[...] = jnp.broadcast_to(h0_ref[...], h_ref.shape)

    xp_ref[pad:pad + tl, :] = x_ref[...]
    xc = cb_ref[...]
    for j in range(conv_w):
        xc = xc + xp_ref[pad - tail + j:pad - tail + j + tl, :] * cw_ref[j:j + 1, :]

    sp_lam = _softplus(-lam_ref[...])
    for n in range(nblk):
        sl = slice(n * bs, (n + 1) * bs)
        xb = xc[:, sl]
        xb16 = xb.astype(BF16)
        r = _sigmoid(jnp.dot(xb16, wr_ref[n], preferred_element_type=F32) + br_ref[:, sl])
        i = _sigmoid(jnp.dot(xb16, wi_ref[n], preferred_element_type=F32) + bi_ref[:, sl])
        log_a = -LRU_C * r * sp_lam[:, sl]
        a = jnp.exp(log_a)
        a_ref[0:tl, sl] = a
        u_ref[0:tl, sl] = jnp.sqrt(-jnp.tanh(log_a) * (a * a + 1.0)) * (i * xb)

    srow = lax.broadcasted_iota(jnp.int32, (SUBLANES, 1), 0)

    def tile_scan(base, hprev):
        a = a_ref[pl.ds(base, SUBLANES), :]
        u = u_ref[pl.ds(base, SUBLANES), :]
        for d in (1, 2, 4):
            a_s = jnp.where(srow >= d, pltpu.roll(a, d, 0), 1.0)
            u_s = jnp.where(srow >= d, pltpu.roll(u, d, 0), 0.0)
            u = u + a * u_s
            a = a * a_s
        hs = u + a * hprev
        a_ref[pl.ds(base, SUBLANES), :] = hs
        return hs

    n8 = -(-tl // SUBLANES)

    def body(k, hprev):
        base = pl.multiple_of(k * SUBLANES, SUBLANES)
        hs = tile_scan(base, hprev)
        return jnp.broadcast_to(hs[SUBLANES - 1:SUBLANES, :], hs.shape)

    if n8 > 1:
        hlast = lax.fori_loop(0, n8 - 1, body, h_ref[...])
    else:
        hlast = h_ref[...]
    hs_last = tile_scan((n8 - 1) * SUBLANES, hlast)
    lr_full = (tl - 1) % SUBLANES
    h_ref[...] = jnp.broadcast_to(hs_last[lr_full:lr_full + 1, :], h_ref.shape)

    gg = g_ref[...]
    o_ref[...] = (a_ref[0:tl, :] * (gg * _sigmoid(gg))).astype(o_ref.dtype)

    @pl.when(ti == nt - 1)
    def _():
        lr = (n_last - 1) % SUBLANES
        hfin_ref[...] = hs_last[lr:lr + 1, :]
        bufnew_ref[...] = xp_ref[pad + n_last - tail:pad + n_last, :]

    @pl.when(ti < nt - 1)
    def _():
        keep = xp_ref[tl:tl + pad, :]
        xp_ref[0:pad, :] = keep


def _lru(proj3, buf0, h0, cw_all, cb_all, wr_all, br_all, wi_all, bi_all, lam_all, layer, tl, n_last, gs, cols):
    bsz, lpad, _ = proj3.shape
    conv_w, w = cw_all.shape[1:]
    nblk, bs = wr_all.shape[1:3]
    cx, cg = (x // w for x in cols)
    nt = lpad // tl
    assert (n_last == tl or nt == 1) and bsz % gs == 0
    tlp = -(-tl // SUBLANES) * SUBLANES
    kern = functools.partial(_lru_kernel, gs=gs, tl=tl, n_last=n_last, conv_w=conv_w, nblk=nblk, bs=bs)
    lead = buf0.ndim - 3

    def st_map(b, t):
        return ((layer,) if lead == 1 else ()) + (b, 0, 0)

    def st_block(rows):
        return ((None,) if lead == 1 else ()) + (gs, rows, w)

    def pspec(rows):
        return pl.BlockSpec((None, rows, w), lambda b, t: (layer, 0, 0))

    def wspec():
        return pl.BlockSpec((None, nblk, bs, bs), lambda b, t: (layer, 0, 0, 0))

    return pl.pallas_call(
        kern,
        grid=(bsz // gs, nt),
        in_specs=[pl.BlockSpec((gs, tl, w), lambda b, t: (b, t, cx)),
                  pl.BlockSpec((gs, tl, w), lambda b, t: (b, t, cg)),
                  pl.BlockSpec(st_block(conv_w - 1), st_map),
                  pl.BlockSpec(st_block(1), st_map),
                  pspec(conv_w), pspec(1), wspec(), pspec(1), wspec(), pspec(1), pspec(1)],
        out_specs=[pl.BlockSpec((gs, tl, w), lambda b, t: (b, t, 0)),
                   pl.BlockSpec((gs, 1, w), lambda b, t: (b, 0, 0)),
                   pl.BlockSpec((gs, conv_w - 1, w), lambda b, t: (b, 0, 0))],
        out_shape=[jax.ShapeDtypeStruct((bsz, lpad, w), BF16),
                   jax.ShapeDtypeStruct((bsz, 1, w), F32),
                   jax.ShapeDtypeStruct((bsz, conv_w - 1, w), F32)],
        scratch_shapes=[pltpu.VMEM((gs, tlp + 2 * SUBLANES, w), F32),
                        pltpu.VMEM((gs, tlp, w), F32),
                        pltpu.VMEM((gs, tlp, w), F32),
                        pltpu.VMEM((gs, SUBLANES, w), F32)],
        compiler_params=_cparams(("parallel", "arbitrary")),
        name="lru",
    )(proj3, proj3, buf0, h0, cw_all, cb_all, wr_all, br_all, wi_all, bi_all, lam_all)


def _tri_ext(tk):
    s1 = np.arange(tk)[:, None]
    s0 = np.arange(tk)[None, :]
    return np.concatenate([(s1 > s0), np.ones((tk, tk), bool)], axis=1).astype(np.float32)


def _sb_terms(z2):
    sp = jnp.log2(1.0 + jnp.exp2(-jnp.abs(z2))) + jnp.maximum(z2, 0.0)
    return sp, z2 - sp


def _sbp_kernel(bias_ref, q_ref, k_ref, v_ref, g_ref, tri_ref, o_ref, k16_ref, v16_ref, *, tq, dh, nq, group, unroll):
    h = pl.program_id(1)
    qscale = dh ** -0.5 * LOG2E
    bias2 = bias_ref[h] * LOG2E
    tri = tri_ref[...]
    for j in range(nq):
        p = nq - 1 - j
        k16_ref[p * tq:(p + 1) * tq, :] = k_ref[j * tq:(j + 1) * tq, :].astype(BF16)
        v16_ref[p * tq:(p + 1) * tq, :] = v_ref[j * tq:(j + 1) * tq, :].astype(BF16)
    cmr = lax.broadcasted_iota(jnp.int32, (tq, tq), 1) - lax.broadcasted_iota(jnp.int32, (tq, tq), 0)

    def q_block(qi, first, nkb):
        r0 = pl.multiple_of(qi * tq, tq)
        lo = (nq - nkb) * tq
        q = (q_ref[pl.ds(r0, tq), :] * qscale).astype(BF16)
        z2 = lax.dot_general(q, k16_ref[lo:nq * tq, :], _NT, preferred_element_type=F32) + bias2
        sp, ls = _sb_terms(z2)
        masks = {}
        sp_blocks = []
        for p in range(nkb):
            j = nkb - 1 - p
            blk = sp[:, p * tq:(p + 1) * tq]
            if j >= first:
                masks[p] = cmr < (qi - j) * tq
                blk = jnp.where(masks[p], blk, 0.0)
            sp_blocks.append(blk.astype(BF16))
        ext = jnp.dot(jnp.concatenate(sp_blocks, axis=0), tri, preferred_element_type=F32)
        run = jnp.zeros((tq, tq), F32)
        w_blocks = []
        for p in range(nkb):
            e = ext[p * tq:(p + 1) * tq]
            wp = jnp.exp2(ls[:, p * tq:(p + 1) * tq] - e[:, :tq] - run)
            if p in masks:
                wp = jnp.where(masks[p], wp, 0.0)
            w_blocks.append(wp.astype(BF16))
            run = run + e[:, tq:]
        acc = jnp.dot(jnp.concatenate(w_blocks, axis=1), v16_ref[lo:nq * tq, :], preferred_element_type=F32)
        gg = g_ref[pl.ds(r0, tq), :]
        o_ref[pl.ds(r0, tq), :] = (acc * (gg * _sigmoid(gg))).astype(o_ref.dtype)

    for first in range(0, nq, group):
        nkb = first + group

        def body(i, carry, first=first, nkb=nkb):
            for u in range(unroll):
                q_block(first + i * unroll + u, first, nkb)
            return carry

        lax.fori_loop(0, group // unroll, body, 0)


def _sb_prompt(proj3, bias_all, layer, heads, dh, tq, group, unroll, cols):
    bsz, seqlen, _ = proj3.shape
    nq = seqlen // tq
    assert nq % group == 0 and group % unroll == 0
    cq, ck, cv, cg = (x // dh for x in cols)
    tri = jnp.asarray(_tri_ext(tq)).astype(BF16)
    kern = functools.partial(_sbp_kernel, tq=tq, dh=dh, nq=nq, group=group, unroll=unroll)

    def colspec(c0):
        return pl.BlockSpec((None, seqlen, dh), lambda b, h: (b, 0, c0 + h))

    return pl.pallas_call(
        kern,
        grid=(bsz, heads),
        in_specs=[pl.BlockSpec(memory_space=pltpu.SMEM),
                  colspec(cq), colspec(ck), colspec(cv), colspec(cg),
                  pl.BlockSpec(tri.shape, lambda b, h: (0, 0))],
        out_specs=pl.BlockSpec((None, seqlen, dh), lambda b, h: (b, 0, h)),
        out_shape=jax.ShapeDtypeStruct((bsz, seqlen, heads * dh), BF16),
        scratch_shapes=[pltpu.VMEM((seqlen, dh), BF16), pltpu.VMEM((seqlen, dh), BF16)],
        compiler_params=_cparams(("parallel", "parallel")),
        name="sb_prompt",
    )(bias_all[layer], proj3, proj3, proj3, proj3, tri)


def _sbs_kernel(pt_ref, bias_ref, q_ref, kc_ref, vc_ref, g_ref, *rest, heads, dh, n_valid, page, pps):
    kp_refs, vp_refs = rest[:pps], rest[pps:2 * pps]
    tri_ref, o_ref, qbd_ref, k16_ref, v16_ref, acc_ref, run_ref = rest[2 * pps:]
    j = pl.program_id(1)
    nr = SUBLANES
    hr = heads * nr
    w = heads * dh
    tri = tri_ref[...]
    bias2 = jnp.concatenate([jnp.full((nr, page), bias_ref[h] * LOG2E, F32) for h in range(heads)], axis=0)

    def sweep(qbd, k16, v16, mask, acc, run):
        nb = k16.shape[0] // page
        z2 = lax.dot_general(qbd, k16, _NT, preferred_element_type=F32)
        sps, lss = [], []
        for p in range(nb):
            sp, ls = _sb_terms(z2[:, p * page:(p + 1) * page] + bias2)
            if mask is not None:
                sp = jnp.where(mask, sp, 0.0)
            sps.append(sp.astype(BF16))
            lss.append(ls)
        ext = jnp.dot(jnp.concatenate(sps, axis=0), tri, preferred_element_type=F32)
        ws = []
        for p in range(nb):
            e = ext[p * hr:(p + 1) * hr]
            wp = jnp.exp2(lss[p] - e[:, :page] - run)
            if mask is not None:
                wp = jnp.where(mask, wp, 0.0)
            ws.append(wp.astype(BF16))
            run = run + e[:, page:]
        return acc + jnp.dot(jnp.concatenate(ws, axis=1), v16, preferred_element_type=F32), run

    @pl.when(j == 0)
    def _():
        rh = lax.broadcasted_iota(jnp.int32, (hr, w), 0) // nr
        ch = lax.broadcasted_iota(jnp.int32, (hr, w), 1) // dh
        qt = jnp.concatenate([q_ref[...] * (dh ** -0.5 * LOG2E)] * heads, axis=0)
        qbd_ref[...] = jnp.where(rh == ch, qt, 0.0).astype(BF16)
        zpad = jnp.zeros((page - nr, w), F32)
        colp = lax.broadcasted_iota(jnp.int32, (hr, page), 1)
        trow = lax.broadcasted_iota(jnp.int32, (hr, page), 0) & (nr - 1)
        kcur = jnp.concatenate([kc_ref[...], zpad], axis=0).astype(BF16)
        vcur = jnp.concatenate([vc_ref[...], zpad], axis=0).astype(BF16)
        mask = (colp < trow) & (colp < n_valid)
        acc, run = sweep(qbd_ref[...], kcur, vcur, mask, jnp.zeros((hr, w), F32), jnp.zeros((hr, page), F32))
        acc_ref[...] = acc
        run_ref[...] = run

    for p in range(pps):
        for h in range(heads):
            rows, cols = slice(p * page, (p + 1) * page), slice(h * dh, (h + 1) * dh)
            k16_ref[rows, cols] = kp_refs[p][pl.ds(h, page, stride=heads), :].astype(BF16)
            v16_ref[rows, cols] = vp_refs[p][pl.ds(h, page, stride=heads), :].astype(BF16)
    acc, run = sweep(qbd_ref[...], k16_ref[...], v16_ref[...], None, acc_ref[...], run_ref[...])
    acc_ref[...] = acc
    run_ref[...] = run

    @pl.when(j == pl.num_programs(1) - 1)
    def _():
        for h in range(heads):
            sl = slice(h * dh, (h + 1) * dh)
            gg = g_ref[:, sl]
            o_ref[:, sl] = (acc[h * nr:(h + 1) * nr, sl] * (gg * _sigmoid(gg))).astype(o_ref.dtype)


def _sb_sample(proj3, cache_k, cache_v, page_table, bias_all, layer, n_valid, pps, cols):
    bsz, lpad, _ = proj3.shape
    assert lpad == SUBLANES
    depth, n_phys, page, heads, dh = cache_k.shape
    n_pages = page_table.shape[1]
    assert n_pages % pps == 0
    w = heads * dh
    cq, ck, cv, cg = (x // w for x in cols)
    kp = cache_k.reshape(depth, n_phys, page * heads, dh)
    vp = cache_v.reshape(depth, n_phys, page * heads, dh)
    tri = jnp.asarray(_tri_ext(page)).astype(BF16)
    kern = functools.partial(_sbs_kernel, heads=heads, dh=dh, n_valid=n_valid, page=page, pps=pps)

    def colspec(cb):
        return pl.BlockSpec((None, lpad, w), lambda b, j, pt: (b, 0, cb))

    def pagespec(p):
        return pl.BlockSpec((None, None, page * heads, dh),
                            lambda b, j, pt: (layer, pt[b, n_pages - 1 - (j * pps + p)], 0, 0))

    grid_spec = pltpu.PrefetchScalarGridSpec(
        num_scalar_prefetch=1,
        grid=(bsz, n_pages // pps),
        in_specs=[pl.BlockSpec(memory_space=pltpu.SMEM),
                  colspec(cq), colspec(ck), colspec(cv), colspec(cg)]
                 + [pagespec(p) for p in range(pps)] + [pagespec(p) for p in range(pps)]
                 + [pl.BlockSpec(tri.shape, lambda b, j, pt: (0, 0))],
        out_specs=pl.BlockSpec((None, lpad, w), lambda b, j, pt: (b, 0, 0)),
        scratch_shapes=[pltpu.VMEM((heads * SUBLANES, w), BF16),
                        pltpu.VMEM((pps * page, w), BF16), pltpu.VMEM((pps * page, w), BF16),
                        pltpu.VMEM((heads * SUBLANES, w), F32), pltpu.VMEM((heads * SUBLANES, page), F32)],
    )
    return pl.pallas_call(
        kern,
        grid_spec=grid_spec,
        out_shape=jax.ShapeDtypeStruct((bsz, lpad, w), BF16),
        compiler_params=_cparams(("parallel", "arbitrary")),
        name="sb_sample",
    )(page_table, bias_all[layer], proj3, proj3, proj3, proj3, *([kp] * pps), *([vp] * pps), tri)


def _merge_kernel(oa_ref, ob_ref, oc_ref, ga_ref, gb_ref, gc_ref, x_ref, wa_ref, wb_ref, wc_ref, wo_ref, gp_ref,
                  y_ref):
    merged = (_sigmoid(ga_ref[...]) * jnp.dot(oa_ref[...], wa_ref[...], preferred_element_type=F32)
              + _sigmoid(gb_ref[...]) * jnp.dot(ob_ref[...], wb_ref[...], preferred_element_type=F32)
              + _sigmoid(gc_ref[...]) * jnp.dot(oc_ref[...], wc_ref[...], preferred_element_type=F32))
    out = jnp.dot(merged.astype(BF16), wo_ref[...], preferred_element_type=F32)
    ms = jnp.mean(out * out, axis=-1, keepdims=True)
    y_ref[...] = x_ref[...] + out * lax.rsqrt(ms + EPS) * gp_ref[...]


def _merge(oa, ob, oc, proj, gate_col, x2d, wa_all, wb_all, wc_all, wo_all, gp_all, layer, bm):
    m, d = x2d.shape
    wbr = oa.shape[1]
    assert gate_col % d == 0

    def ospec():
        return pl.BlockSpec((bm, wbr), lambda i: (i, 0))

    def gspec(k):
        return pl.BlockSpec((bm, d), lambda i: (i, gate_col // d + k))

    def wspec(rows):
        return pl.BlockSpec((None, rows, d), lambda i: (layer, 0, 0), pipeline_mode=pl.Buffered(1))

    return pl.pallas_call(
        _merge_kernel,
        grid=(m // bm,),
        in_specs=[ospec(), ospec(), ospec(), gspec(0), gspec(1), gspec(2),
                  pl.BlockSpec((bm, d), lambda i: (i, 0)),
                  wspec(wbr), wspec(wbr), wspec(wbr), wspec(d),
                  pl.BlockSpec((None, 1, d), lambda i: (layer, 0, 0))],
        out_specs=pl.BlockSpec((bm, d), lambda i: (i, 0)),
        out_shape=jax.ShapeDtypeStruct((m, d), F32),
        compiler_params=_cparams(("parallel",)),
        name="merge",
    )(oa, ob, oc, proj, proj, proj, x2d, wa_all, wb_all, wc_all, wo_all, gp_all)


def _kvout_kernel(*refs, depth, heads, dh, tm):
    k_ins, v_ins = refs[:depth], refs[depth:2 * depth]
    ko_ref, vo_ref = refs[2 * depth:]
    layer = pl.program_id(0)
    for l in range(depth):
        @pl.when(layer == l)
        def _(l=l):
            for h in range(heads):
                ko_ref[pl.ds(h, tm, stride=heads), :] = k_ins[l][:, h * dh:(h + 1) * dh]
                vo_ref[pl.ds(h, tm, stride=heads), :] = v_ins[l][:, h * dh:(h + 1) * dh]


def _kvout(projs, heads, dh, ck, cv, tm):
    depth = len(projs)
    m = projs[0].shape[0]
    w = heads * dh
    nt = m // tm

    def inspec(l, col):
        return pl.BlockSpec((tm, w), lambda d_, i: (jnp.where(d_ == l, i, jnp.where(d_ < l, 0, nt - 1)), col // w))

    out_spec = pl.BlockSpec((None, tm * heads, dh), lambda d_, i: (d_, i, 0))
    out_sds = jax.ShapeDtypeStruct((depth, m * heads, dh), F32)
    return pl.pallas_call(
        functools.partial(_kvout_kernel, depth=depth, heads=heads, dh=dh, tm=tm),
        grid=(depth, nt),
        in_specs=[inspec(l, ck) for l in range(depth)] + [inspec(l, cv) for l in range(depth)],
        out_specs=[out_spec, out_spec],
        out_shape=[out_sds, out_sds],
        compiler_params=_cparams(("arbitrary", "arbitrary")),
        name="kvout",
    )(*projs, *projs)


def _pick(n, prefs):
    for p in prefs:
        if n % p == 0:
            return p
    return n


def kernel(x_prompt, x_sample, cache_k, cache_v, state_hgrn, state_lru_h, state_conv, page_table, norm_pre, norm_post, w_in, hgrn_lb_logits, hgrn_norm, lru_conv_w, lru_conv_b, lru_w_r, lru_b_r, lru_w_i, lru_b_i, lru_lambda, sb_bias, w_branch_a, w_branch_b, w_branch_c, w_out):
    depth, d, n_cols = w_in.shape
    pb, seq, _ = x_prompt.shape
    sb, dec_seq, _ = x_sample.shape
    _, _, a_heads, a_dk, a_dv = state_hgrn.shape
    a_kw, a_w = a_heads * a_dk, a_heads * a_dv
    b_w = state_lru_h.shape[2]
    conv_w = lru_conv_w.shape[1]
    _, _, page, c_heads, c_dh = cache_k.shape
    c_w = c_heads * c_dh
    sizes = (a_kw, a_kw, a_w, a_w, b_w, b_w, c_w, c_w, c_w, c_w, d, d, d)
    assert sum(sizes) == n_cols and a_kw == a_w
    offs = np.concatenate([[0], np.cumsum(sizes)]).tolist()
    (o_aq, o_af, o_ai, o_ag, o_bx, o_bg, o_cq, o_ck, o_cv, o_cg, o_ma) = offs[:11]

    lbs = jnp.cumsum(jax.nn.softmax(hgrn_lb_logits.astype(F32), axis=0), axis=0)
    lbs = lbs - lbs[0:1]
    loglb = jnp.log(lbs)[:, None, :]
    log1mlb = jnp.log1p(-lbs)[:, None, :]
    omlb = (1.0 - lbs)[:, None, :]
    r3 = lambda p: p.astype(F32)[:, None, :]
    norm_pre3, norm_post3, hgrn_norm3 = r3(norm_pre), r3(norm_post), r3(hgrn_norm)
    conv_b3, b_r3, b_i3, lam3 = r3(lru_conv_b), r3(lru_b_r), r3(lru_b_i), r3(lru_lambda)
    conv_w3 = lru_conv_w.astype(F32)
    w_in16 = w_in.astype(BF16)
    w_r16, w_i16 = lru_w_r.astype(BF16), lru_w_i.astype(BF16)
    w_a16, w_b16, w_c16, w_o16 = (t.astype(BF16) for t in (w_branch_a, w_branch_b, w_branch_c, w_out))
    sbias = sb_bias.astype(F32)

    lpad = SUBLANES
    assert dec_seq <= lpad and dec_seq >= conv_w - 1
    xs_pad = jnp.pad(x_sample, ((0, 0), (0, lpad - dec_seq), (0, 0)))

    mp, ms = pb * seq, sb * lpad
    bn = _pick(n_cols, (1024, 512, 256, 128))
    hc = _pick(seq, (128, 64, 32, 16, 8))
    tl = _pick(seq, (256, 128, 64, 32, 16, 8))
    tq = _pick(seq, (128,))
    sbg = _pick(seq // tq, (2, 1))
    sbu = sbg
    hgs = _pick(sb, (4, 2, 1))
    lgs = _pick(sb, (8, 4, 2, 1))
    pps = _pick(page_table.shape[1], (16, 8, 4, 2, 1))
    hp, hs = x_prompt.reshape(mp, d), xs_pad.reshape(ms, d)
    zs_p = jnp.zeros((pb, a_heads, a_dk, a_dv), F32)
    zh_p = jnp.zeros((pb, 1, b_w), F32)
    zb_p = jnp.zeros((pb, conv_w - 1, b_w), F32)
    h0_s = state_lru_h[:, :, None, :]

    outs = [[] for _ in range(6)]
    projs = ([], [])
    for l in range(depth):
        for grp in range(2):
            if grp == 0:
                x2d, bsz, lp, nv = hp, pb, seq, seq
            else:
                x2d, bsz, lp, nv = hs, sb, lpad, dec_seq
            m = x2d.shape[0]
            bm = _pick(m, (1024, 512, 256, 128, 64, 32, 16, 8))
            proj = _inproj(x2d, norm_pre3, w_in16, l, bm, bn)
            main3 = proj.reshape(bsz, lp, n_cols)
            if grp == 0:
                o_a, s_new = _hgrn(main3, zs_p, loglb, log1mlb, omlb, hgrn_norm3, l, hc, hc, 1,
                                   (o_aq, o_af, o_ai, o_ag))
                o_b, h_new, buf_new = _lru(main3, zb_p, zh_p, conv_w3, conv_b3, w_r16, b_r3, w_i16, b_i3, lam3, l,
                                           tl, tl, 1, (o_bx, o_bg))
                o_c = _sb_prompt(main3, sbias, l, c_heads, c_dh, tq, sbg, sbu, (o_cq, o_ck, o_cv, o_cg))
            else:
                o_a, s_new = _hgrn(main3, state_hgrn, loglb, log1mlb, omlb, hgrn_norm3, l, lpad, nv, hgs,
                                   (o_aq, o_af, o_ai, o_ag))
                o_b, h_new, buf_new = _lru(main3, state_conv, h0_s, conv_w3, conv_b3, w_r16, b_r3, w_i16, b_i3,
                                           lam3, l, lpad, nv, lgs, (o_bx, o_bg))
                o_c = _sb_sample(main3, cache_k, cache_v, page_table, sbias, l, nv, pps, (o_cq, o_ck, o_cv, o_cg))
            bmm = _pick(m, (256, 128, 64, 32, 16, 8))
            y = _merge(o_a.reshape(m, a_w), o_b.reshape(m, b_w), o_c.reshape(m, c_w), proj, o_ma, x2d,
                       w_a16, w_b16, w_c16, w_o16, norm_post3, l, bmm)
            projs[grp].append(proj)
            for lst, val in zip(outs[grp * 3:(grp + 1) * 3], (s_new, h_new[:, 0, :], buf_new)):
                lst.append(val)
            if grp == 0:
                hp = y
            else:
                hs = y
    y_p = hp.reshape(pb, seq, d)
    y_s = hs.reshape(sb, lpad, d)[:, :dec_seq]
    pk, pv = (t.reshape(depth, pb, seq, c_heads, c_dh)
              for t in _kvout(projs[0], c_heads, c_dh, o_ck, o_cv, _pick(mp, (512, 256, 128, 64, 32, 16, 8))))
    sk, sv = (t.reshape(depth, sb, lpad, c_heads, c_dh)[:, :, :dec_seq]
              for t in _kvout(projs[1], c_heads, c_dh, o_ck, o_cv, _pick(ms, (512, 256, 128, 64, 32, 16, 8))))
    ps, ph, pc, ss, sh, sc = (jnp.stack(o) for o in outs)
    return (y_p, y_s, pk, pv, ps, ph, pc, sk, sv, ss, sh, sc)
```

```python
import functools
import math

import numpy as np
import jax
import jax.numpy as jnp
from jax import lax
from jax.experimental import pallas as pl
from jax.experimental.pallas import tpu as pltpu

F32 = jnp.float32
BF16 = jnp.bfloat16
EPS = 1e-6
LRU_C = 8.0
LOG2E = 1.4426950408889634
LANES = 128
SUBLANES = 8
VMEM_LIMIT = 56 * 1024 * 1024

_NT = (((1,), (1,)), ((), ()))
_TN = (((0,), (0,)), ((), ()))


def _cparams(sem):
    return pltpu.CompilerParams(dimension_semantics=sem, vmem_limit_bytes=VMEM_LIMIT)


def _sigmoid(x):
    return 0.5 * jnp.tanh(0.5 * x) + 0.5


def _softplus(x):
    return jnp.maximum(x, 0.0) + jnp.log1p(jnp.exp(-jnp.abs(x)))


def _inproj_kernel(x_ref, g_ref, w_ref, proj_ref, xn_ref):
    @pl.when(pl.program_id(1) == 0)
    def _():
        x = x_ref[...]
        ms = jnp.mean(x * x, axis=-1, keepdims=True)
        xn_ref[...] = (x * lax.rsqrt(ms + EPS) * g_ref[...]).astype(BF16)

    proj_ref[...] = jnp.dot(xn_ref[...], w_ref[...], preferred_element_type=F32)


def _inproj(x2d, g_all, w_all, layer, bm, bn):
    m, d = x2d.shape
    n = w_all.shape[2]
    return pl.pallas_call(
        _inproj_kernel,
        grid=(m // bm, n // bn),
        in_specs=[pl.BlockSpec((bm, d), lambda i, j: (i, 0)),
                  pl.BlockSpec((None, 1, d), lambda i, j: (layer, 0, 0)),
                  pl.BlockSpec((None, d, bn), lambda i, j: (layer, 0, j))],
        out_specs=pl.BlockSpec((bm, bn), lambda i, j: (i, j)),
        out_shape=jax.ShapeDtypeStruct((m, n), F32),
        scratch_shapes=[pltpu.VMEM((bm, d), BF16)],
        compiler_params=_cparams(("parallel", "arbitrary")),
        name="inproj",
    )(x2d, g_all, w_all)


def _hgrn_level_matrix(c):
    nlev = int(math.log2(c))
    assert 1 << nlev == c
    t = np.arange(c)[:, None]
    j = np.arange(c)[None, :]
    mats = []
    for lev in range(nlev):
        m = 2 << lev
        mid = (t // m) * m + m // 2 - 1
        second = (t % m) >= m // 2
        q_side = second & (j > mid) & (j <= t)
        k_side = (~second) & (j > t) & (j <= mid)
        mats.append(q_side | k_side)
    mats.append(j <= t)
    mats.append(j > t)
    return np.concatenate(mats, axis=0).astype(np.float32), nlev


def _hgrn_kernel(q_ref, f_ref, i_ref, g_ref, s0_ref, loglb_ref, log1mlb_ref, omlb_ref, gn_ref, m_ref, *rest,
                 c, heads, dk, n_valid, nlev, gs):
    o_ref, snew_ref, st_ref = rest[-3:]
    ci = pl.program_id(1)

    @pl.when(ci == 0)
    def _():
        for g in range(gs):
            for h in range(heads):
                st_ref[g, h] = s0_ref[g, h].T

    row = lax.broadcasted_iota(jnp.int32, (c, 1), 0)
    rr = lax.broadcasted_iota(jnp.int32, (c, c), 0)
    cc = lax.broadcasted_iota(jnp.int32, (c, c), 1)
    rxc = rr ^ cc
    pair_lev = jnp.full((c, c), -1, jnp.int32)
    for lev in range(nlev):
        pair_lev = jnp.where((rr > cc) & (rxc >= (1 << lev)) & (rxc < (2 << lev)), lev, pair_lev)
    mstack = m_ref[...]

    hsl = [slice(h * dk, (h + 1) * dk) for h in range(heads)]
    for g in range(gs):
        zf = f_ref[g]
        e = jnp.exp(-jnp.abs(zf))
        r = 1.0 / (1.0 + e)
        logsig = jnp.minimum(zf, 0.0) - jnp.log1p(e)
        a = loglb_ref[...]
        b = log1mlb_ref[...] + logsig
        log_f = jnp.maximum(a, b) + jnp.log1p(jnp.exp(-jnp.abs(a - b)))
        k_in = omlb_ref[...] * jnp.where(zf >= 0, e * r, r)
        aq = q_ref[g]
        q = aq * _sigmoid(aq)
        v16 = i_ref[g].astype(BF16)
        if n_valid < c:
            valid = row < n_valid
            log_f = jnp.where(valid, log_f, 0.0)
            k_in = jnp.where(valid, k_in, 0.0)
        lf_hi = log_f.astype(BF16).astype(F32)
        lf2 = jnp.concatenate([lf_hi, log_f - lf_hi], axis=0).astype(BF16)
        ex = jnp.exp(jnp.dot(mstack, lf2, preferred_element_type=F32))
        cum_x = ex[nlev * c:(nlev + 1) * c]
        qd = (q * cum_x).astype(BF16)
        kd = (k_in * ex[(nlev + 1) * c:(nlev + 2) * c]).astype(BF16)
        qk = q * k_in

        sts = [st_ref[g, h] for h in range(heads)]
        o_inter = [lax.dot_general(qd[:, hsl[h]], sts[h].astype(BF16), _NT, preferred_element_type=F32)
                   for h in range(heads)]
        upd = [lax.dot_general(v16[:, hsl[h]], kd[:, hsl[h]], _TN, preferred_element_type=F32)
               for h in range(heads)]
        als = []
        for lev in range(nlev):
            second = (row & (1 << lev)) != 0
            y = (jnp.where(second, q, k_in) * ex[lev * c:(lev + 1) * c]).astype(BF16)
            als.append([lax.dot_general(y[:, hsl[h]], y[:, hsl[h]], _NT, preferred_element_type=F32)
                        for h in range(heads)])

        ag = g_ref[g]
        gate = gn_ref[...] * (ag * _sigmoid(ag))
        for h in range(heads):
            att = jnp.where(rxc == 0, jnp.sum(qk[:, hsl[h]], axis=-1, keepdims=True), 0.0)
            for lev in range(nlev):
                att = jnp.where(pair_lev == lev, als[lev][h], att)
            o = o_inter[h] + jnp.dot(att.astype(BF16), v16[:, hsl[h]], preferred_element_type=F32)
            st_ref[g, h] = sts[h] * cum_x[c - 1:c, hsl[h]] + upd[h]
            ms = jnp.mean(o * o, axis=-1, keepdims=True)
            o_ref[g, :, hsl[h]] = (o * lax.rsqrt(ms + EPS) * gate[:, hsl[h]]).astype(o_ref.dtype)

    @pl.when(ci == pl.num_programs(1) - 1)
    def _():
        for g in range(gs):
            for h in range(heads):
                snew_ref[g, h] = st_ref[g, h].T


def _hgrn(proj3, s0, s_all, depth, loglb, log1mlb, omlb, gn_all, layer, c, n_valid, gs, cols):
    bsz, lpad, _ = proj3.shape
    _, heads, dk, dv = s0.shape[-4:]
    assert dk == LANES and dv == LANES and bsz % gs == 0
    w = heads * dk
    mstack, nlev = _hgrn_level_matrix(c)
    mstack = jnp.asarray(np.concatenate([mstack, mstack], axis=1)).astype(BF16)
    cq, cf, ci_, cg = (x // w for x in cols)
    kern = functools.partial(_hgrn_kernel, c=c, heads=heads, dk=dk, n_valid=n_valid, nlev=nlev, gs=gs)
    s0_lead = s0.ndim - 4

    def s0_map(b, t):
        return ((layer,) if s0_lead == 1 else ()) + (b, 0, 0, 0)

    s0_block = ((None,) if s0_lead == 1 else ()) + (gs, heads, dk, dv)

    def colspec(cb):
        return pl.BlockSpec((gs, c, w), lambda b, t: (b, t, cb))

    def pspec():
        return pl.BlockSpec((None, 1, w), lambda b, t: (layer, 0, 0))

    in_specs = [colspec(cq), colspec(cf), colspec(ci_), colspec(cg),
                pl.BlockSpec(s0_block, s0_map),
                pspec(), pspec(), pspec(), pspec(),
                pl.BlockSpec(mstack.shape, lambda b, t: (0, 0))]
    args = [proj3, proj3, proj3, proj3, s0, loglb, log1mlb, omlb, gn_all, mstack]
    aliases = {}
    if s_all is not None:
        in_specs.append(pl.BlockSpec(memory_space=pl.ANY))
        args.append(s_all)
        aliases = {len(args) - 1: 1}
    return pl.pallas_call(
        kern,
        grid=(bsz // gs, lpad // c),
        in_specs=in_specs,
        out_specs=[pl.BlockSpec((gs, c, w), lambda b, t: (b, t, 0)),
                   pl.BlockSpec((None, gs, heads, dk, dv), lambda b, t: (layer, b, 0, 0, 0))],
        out_shape=[jax.ShapeDtypeStruct((bsz, lpad, w), BF16),
                   jax.ShapeDtypeStruct((depth, bsz, heads, dk, dv), F32)],
        input_output_aliases=aliases,
        scratch_shapes=[pltpu.VMEM((gs, heads, dv, dk), F32)],
        compiler_params=_cparams(("parallel", "arbitrary")),
        name="hgrn",
    )(*args)


def _lru_kernel(*refs, gs, **kw):
    seq_in, params, seq_rest = refs[:4], refs[4:11], refs[11:]
    for g in range(gs):
        _lru_seq(*[r.at[g] for r in seq_in], *params, *[r.at[g] for r in seq_rest], **kw)


def _lru_seq(x_ref, g_ref, buf0_ref, h0_ref, cw_ref, cb_ref, wr_ref, br_ref, wi_ref, bi_ref, lam_ref,
             o_ref, hfin_ref, bufnew_ref, xp_ref, a_ref, u_ref, h_ref, *, tl, n_last, conv_w, nblk, bs):
    ti = pl.program_id(1)
    nt = pl.num_programs(1)
    pad = SUBLANES
    tail = conv_w - 1

    @pl.when(ti == 0)
    def _():
        xp_ref[0:pad, :] = jnp.zeros((pad, xp_ref.shape[1]), F32)
        xp_ref[pad - tail:pad, :] = buf0_ref[...]
        h_ref[...] = jnp.broadcast_to(h0_ref[...], h_ref.shape)

    xp_ref[pad:pad + tl, :] = x_ref[...]
    xc = cb_ref[...]
    for j in range(conv_w):
        xc = xc + xp_ref[pad - tail + j:pad - tail + j + tl, :] * cw_ref[j:j + 1, :]

    sp_lam = _softplus(-lam_ref[...])
    for n in range(nblk):
        sl = slice(n * bs, (n + 1) * bs)
        xb = xc[:, sl]
        xb16 = xb.astype(BF16)
        r = _sigmoid(jnp.dot(xb16, wr_ref[n], preferred_element_type=F32) + br_ref[:, sl])
        i = _sigmoid(jnp.dot(xb16, wi_ref[n], preferred_element_type=F32) + bi_ref[:, sl])
        log_a = -LRU_C * r * sp_lam[:, sl]
        a = jnp.exp(log_a)
        a_ref[0:tl, sl] = a
        u_ref[0:tl, sl] = jnp.sqrt(-jnp.tanh(log_a) * (a * a + 1.0)) * (i * xb)

    srow = lax.broadcasted_iota(jnp.int32, (SUBLANES, 1), 0)

    def tile_scan(base, hprev):
        a = a_ref[pl.ds(base, SUBLANES), :]
        u = u_ref[pl.ds(base, SUBLANES), :]
        for d in (1, 2, 4):
            a_s = jnp.where(srow >= d, pltpu.roll(a, d, 0), 1.0)
            u_s = jnp.where(srow >= d, pltpu.roll(u, d, 0), 0.0)
            u = u + a * u_s
            a = a * a_s
        hs = u + a * hprev
        a_ref[pl.ds(base, SUBLANES), :] = hs
        return hs

    n8 = -(-tl // SUBLANES)

    def body(k, hprev):
        base = pl.multiple_of(k * SUBLANES, SUBLANES)
        hs = tile_scan(base, hprev)
        return jnp.broadcast_to(hs[SUBLANES - 1:SUBLANES, :], hs.shape)

    if n8 > 1:
        hlast = lax.fori_loop(0, n8 - 1, body, h_ref[...])
    else:
        hlast = h_ref[...]
    hs_last = tile_scan((n8 - 1) * SUBLANES, hlast)
    lr_full = (tl - 1) % SUBLANES
    h_ref[...] = jnp.broadcast_to(hs_last[lr_full:lr_full + 1, :], h_ref.shape)

    gg = g_ref[...]
    o_ref[...] = (a_ref[0:tl, :] * (gg * _sigmoid(gg))).astype(o_ref.dtype)

    @pl.when(ti == nt - 1)
    def _():
        lr = (n_last - 1) % SUBLANES
        hfin_ref[...] = hs_last[lr:lr + 1, :]
        bufnew_ref[...] = xp_ref[pad + n_last - tail:pad + n_last, :]

    @pl.when(ti < nt - 1)
    def _():
        keep = xp_ref[tl:tl + pad, :]
        xp_ref[0:pad, :] = keep


def _lru(proj3, buf0, h0, cw_all, cb_all, wr_all, br_all, wi_all, bi_all, lam_all, layer, tl, n_last, gs, cols):
    bsz, lpad, _ = proj3.shape
    conv_w, w = cw_all.shape[1:]
    nblk, bs = wr_all.shape[1:3]
    cx, cg = (x // w for x in cols)
    nt = lpad // tl
    assert (n_last == tl or nt == 1) and bsz % gs == 0
    tlp = -(-tl // SUBLANES) * SUBLANES
    kern = functools.partial(_lru_kernel, gs=gs, tl=tl, n_last=n_last, conv_w=conv_w, nblk=nblk, bs=bs)
    lead = buf0.ndim - 3

    def st_map(b, t):
        return ((layer,) if lead == 1 else ()) + (b, 0, 0)

    def st_block(rows):
        return ((None,) if lead == 1 else ()) + (gs, rows, w)

    def pspec(rows):
        return pl.BlockSpec((None, rows, w), lambda b, t: (layer, 0, 0))

    def wspec():
        return pl.BlockSpec((None, nblk, bs, bs), lambda b, t: (layer, 0, 0, 0))

    return pl.pallas_call(
        kern,
        grid=(bsz // gs, nt),
        in_specs=[pl.BlockSpec((gs, tl, w), lambda b, t: (b, t, cx)),
                  pl.BlockSpec((gs, tl, w), lambda b, t: (b, t, cg)),
                  pl.BlockSpec(st_block(conv_w - 1), st_map),
                  pl.BlockSpec(st_block(1), st_map),
                  pspec(conv_w), pspec(1), wspec(), pspec(1), wspec(), pspec(1), pspec(1)],
        out_specs=[pl.BlockSpec((gs, tl, w), lambda b, t: (b, t, 0)),
                   pl.BlockSpec((gs, 1, w), lambda b, t: (b, 0, 0)),
                   pl.BlockSpec((gs, conv_w - 1, w), lambda b, t: (b, 0, 0))],
        out_shape=[jax.ShapeDtypeStruct((bsz, lpad, w), BF16),
                   jax.ShapeDtypeStruct((bsz, 1, w), F32),
                   jax.ShapeDtypeStruct((bsz, conv_w - 1, w), F32)],
        scratch_shapes=[pltpu.VMEM((gs, tlp + 2 * SUBLANES, w), F32),
                        pltpu.VMEM((gs, tlp, w), F32),
                        pltpu.VMEM((gs, tlp, w), F32),
                        pltpu.VMEM((gs, SUBLANES, w), F32)],
        compiler_params=_cparams(("parallel", "arbitrary")),
        name="lru",
    )(proj3, proj3, buf0, h0, cw_all, cb_all, wr_all, br_all, wi_all, bi_all, lam_all)


def _tri_ext(tk):
    s1 = np.arange(tk)[:, None]
    s0 = np.arange(tk)[None, :]
    return np.concatenate([(s1 > s0), np.ones((tk, tk), bool)], axis=1).astype(np.float32)


def _sb_terms(z2):
    sp = jnp.log2(1.0 + jnp.exp2(-jnp.abs(z2))) + jnp.maximum(z2, 0.0)
    return sp, z2 - sp


def _sbp_kernel(bias_ref, q_ref, k_ref, v_ref, g_ref, tri_ref, o_ref, k16_ref, v16_ref, *, tq, dh, nq, group, unroll):
    h = pl.program_id(1)
    qscale = dh ** -0.5 * LOG2E
    bias2 = bias_ref[h] * LOG2E
    tri = tri_ref[...]
    for j in range(nq):
        p = nq - 1 - j
        k16_ref[p * tq:(p + 1) * tq, :] = k_ref[j * tq:(j + 1) * tq, :].astype(BF16)
        v16_ref[p * tq:(p + 1) * tq, :] = v_ref[j * tq:(j + 1) * tq, :].astype(BF16)
    cmr = lax.broadcasted_iota(jnp.int32, (tq, tq), 1) - lax.broadcasted_iota(jnp.int32, (tq, tq), 0)

    def q_block(qi, first, nkb):
        r0 = pl.multiple_of(qi * tq, tq)
        lo = (nq - nkb) * tq
        q = (q_ref[pl.ds(r0, tq), :] * qscale).astype(BF16)
        z2 = lax.dot_general(q, k16_ref[lo:nq * tq, :], _NT, preferred_element_type=F32) + bias2
        sp, ls = _sb_terms(z2)
        masks = {}
        sp_blocks = []
        for p in range(nkb):
            j = nkb - 1 - p
            blk = sp[:, p * tq:(p + 1) * tq]
            if j >= first:
                masks[p] = cmr < (qi - j) * tq
                blk = jnp.where(masks[p], blk, 0.0)
            sp_blocks.append(blk.astype(BF16))
        ext = jnp.dot(jnp.concatenate(sp_blocks, axis=0), tri, preferred_element_type=F32)
        run = jnp.zeros((tq, tq), F32)
        w_blocks = []
        for p in range(nkb):
            e = ext[p * tq:(p + 1) * tq]
            wp = jnp.exp2(ls[:, p * tq:(p + 1) * tq] - e[:, :tq] - run)
            if p in masks:
                wp = jnp.where(masks[p], wp, 0.0)
            w_blocks.append(wp.astype(BF16))
            run = run + e[:, tq:]
        acc = jnp.dot(jnp.concatenate(w_blocks, axis=1), v16_ref[lo:nq * tq, :], preferred_element_type=F32)
        gg = g_ref[pl.ds(r0, tq), :]
        o_ref[pl.ds(r0, tq), :] = (acc * (gg * _sigmoid(gg))).astype(o_ref.dtype)

    for first in range(0, nq, group):
        nkb = first + group

        def body(i, carry, first=first, nkb=nkb):
            for u in range(unroll):
                q_block(first + i * unroll + u, first, nkb)
            return carry

        lax.fori_loop(0, group // unroll, body, 0)


def _sb_prompt(proj3, bias_all, layer, heads, dh, tq, group, unroll, cols):
    bsz, seqlen, _ = proj3.shape
    nq = seqlen // tq
    assert nq % group == 0 and group % unroll == 0
    cq, ck, cv, cg = (x // dh for x in cols)
    tri = jnp.asarray(_tri_ext(tq)).astype(BF16)
    kern = functools.partial(_sbp_kernel, tq=tq, dh=dh, nq=nq, group=group, unroll=unroll)

    def colspec(c0):
        return pl.BlockSpec((None, seqlen, dh), lambda b, h: (b, 0, c0 + h))

    return pl.pallas_call(
        kern,
        grid=(bsz, heads),
        in_specs=[pl.BlockSpec(memory_space=pltpu.SMEM),
                  colspec(cq), colspec(ck), colspec(cv), colspec(cg),
                  pl.BlockSpec(tri.shape, lambda b, h: (0, 0))],
        out_specs=pl.BlockSpec((None, seqlen, dh), lambda b, h: (b, 0, h)),
        out_shape=jax.ShapeDtypeStruct((bsz, seqlen, heads * dh), BF16),
        scratch_shapes=[pltpu.VMEM((seqlen, dh), BF16), pltpu.VMEM((seqlen, dh), BF16)],
        compiler_params=_cparams(("parallel", "parallel")),
        name="sb_prompt",
    )(bias_all[layer], proj3, proj3, proj3, proj3, tri)


def _sbs_kernel(pt_ref, bias_ref, q_ref, kc_ref, vc_ref, g_ref, *rest, heads, dh, n_valid, page, pps):
    kp_refs, vp_refs = rest[:pps], rest[pps:2 * pps]
    tri_ref, o_ref, qbd_ref, k16_ref, v16_ref, acc_ref, run_ref = rest[2 * pps:]
    j = pl.program_id(1)
    nr = SUBLANES
    hr = heads * nr
    w = heads * dh
    tri = tri_ref[...]
    bias2 = jnp.concatenate([jnp.full((nr, page), bias_ref[h] * LOG2E, F32) for h in range(heads)], axis=0)

    def sweep(qbd, k16, v16, mask, acc, run):
        nb = k16.shape[0] // page
        z2 = lax.dot_general(qbd, k16, _NT, preferred_element_type=F32)
        sps, lss = [], []
        for p in range(nb):
            sp, ls = _sb_terms(z2[:, p * page:(p + 1) * page] + bias2)
            if mask is not None:
                sp = jnp.where(mask, sp, 0.0)
            sps.append(sp.astype(BF16))
            lss.append(ls)
        ext = jnp.dot(jnp.concatenate(sps, axis=0), tri, preferred_element_type=F32)
        ws = []
        for p in range(nb):
            e = ext[p * hr:(p + 1) * hr]
            wp = jnp.exp2(lss[p] - e[:, :page] - run)
            if mask is not None:
                wp = jnp.where(mask, wp, 0.0)
            ws.append(wp.astype(BF16))
            run = run + e[:, page:]
        return acc + jnp.dot(jnp.concatenate(ws, axis=1), v16, preferred_element_type=F32), run

    @pl.when(j == 0)
    def _():
        rh = lax.broadcasted_iota(jnp.int32, (hr, w), 0) // nr
        ch = lax.broadcasted_iota(jnp.int32, (hr, w), 1) // dh
        qt = jnp.concatenate([q_ref[...] * (dh ** -0.5 * LOG2E)] * heads, axis=0)
        qbd_ref[...] = jnp.where(rh == ch, qt, 0.0).astype(BF16)
        zpad = jnp.zeros((page - nr, w), F32)
        colp = lax.broadcasted_iota(jnp.int32, (hr, page), 1)
        trow = lax.broadcasted_iota(jnp.int32, (hr, page), 0) & (nr - 1)
        kcur = jnp.concatenate([kc_ref[...], zpad], axis=0).astype(BF16)
        vcur = jnp.concatenate([vc_ref[...], zpad], axis=0).astype(BF16)
        mask = (colp < trow) & (colp < n_valid)
        acc, run = sweep(qbd_ref[...], kcur, vcur, mask, jnp.zeros((hr, w), F32), jnp.zeros((hr, page), F32))
        acc_ref[...] = acc
        run_ref[...] = run

    for p in range(pps):
        for h in range(heads):
            rows, cols = slice(p * page, (p + 1) * page), slice(h * dh, (h + 1) * dh)
            k16_ref[rows, cols] = kp_refs[p][pl.ds(h, page, stride=heads), :].astype(BF16)
            v16_ref[rows, cols] = vp_refs[p][pl.ds(h, page, stride=heads), :].astype(BF16)
    acc, run = sweep(qbd_ref[...], k16_ref[...], v16_ref[...], None, acc_ref[...], run_ref[...])
    acc_ref[...] = acc
    run_ref[...] = run

    @pl.when(j == pl.num_programs(1) - 1)
    def _():
        for h in range(heads):
            sl = slice(h * dh, (h + 1) * dh)
            gg = g_ref[:, sl]
            o_ref[:, sl] = (acc[h * nr:(h + 1) * nr, sl] * (gg * _sigmoid(gg))).astype(o_ref.dtype)


def _sb_sample(proj3, cache_k, cache_v, page_table, bias_all, layer, n_valid, pps, cols):
    bsz, lpad, _ = proj3.shape
    assert lpad == SUBLANES
    depth, n_phys, page, heads, dh = cache_k.shape
    n_pages = page_table.shape[1]
    assert n_pages % pps == 0
    w = heads * dh
    cq, ck, cv, cg = (x // w for x in cols)
    kp = cache_k.reshape(depth, n_phys, page * heads, dh)
    vp = cache_v.reshape(depth, n_phys, page * heads, dh)
    tri = jnp.asarray(_tri_ext(page)).astype(BF16)
    kern = functools.partial(_sbs_kernel, heads=heads, dh=dh, n_valid=n_valid, page=page, pps=pps)

    def colspec(cb):
        return pl.BlockSpec((None, lpad, w), lambda b, j, pt: (b, 0, cb))

    def pagespec(p):
        return pl.BlockSpec((None, None, page * heads, dh),
                            lambda b, j, pt: (layer, pt[b, n_pages - 1 - (j * pps + p)], 0, 0))

    grid_spec = pltpu.PrefetchScalarGridSpec(
        num_scalar_prefetch=1,
        grid=(bsz, n_pages // pps),
        in_specs=[pl.BlockSpec(memory_space=pltpu.SMEM),
                  colspec(cq), colspec(ck), colspec(cv), colspec(cg)]
                 + [pagespec(p) for p in range(pps)] + [pagespec(p) for p in range(pps)]
                 + [pl.BlockSpec(tri.shape, lambda b, j, pt: (0, 0))],
        out_specs=pl.BlockSpec((None, lpad, w), lambda b, j, pt: (b, 0, 0)),
        scratch_shapes=[pltpu.VMEM((heads * SUBLANES, w), BF16),
                        pltpu.VMEM((pps * page, w), BF16), pltpu.VMEM((pps * page, w), BF16),
                        pltpu.VMEM((heads * SUBLANES, w), F32), pltpu.VMEM((heads * SUBLANES, page), F32)],
    )
    return pl.pallas_call(
        kern,
        grid_spec=grid_spec,
        out_shape=jax.ShapeDtypeStruct((bsz, lpad, w), BF16),
        compiler_params=_cparams(("parallel", "arbitrary")),
        name="sb_sample",
    )(page_table, bias_all[layer], proj3, proj3, proj3, proj3, *([kp] * pps), *([vp] * pps), tri)


def _merge_kernel(oa_ref, ob_ref, oc_ref, ga_ref, gb_ref, gc_ref, x_ref, wa_ref, wb_ref, wc_ref, wo_ref, gp_ref,
                  y_ref):
    merged = (_sigmoid(ga_ref[...]) * jnp.dot(oa_ref[...], wa_ref[...], preferred_element_type=F32)
              + _sigmoid(gb_ref[...]) * jnp.dot(ob_ref[...], wb_ref[...], preferred_element_type=F32)
              + _sigmoid(gc_ref[...]) * jnp.dot(oc_ref[...], wc_ref[...], preferred_element_type=F32))
    out = jnp.dot(merged.astype(BF16), wo_ref[...], preferred_element_type=F32)
    ms = jnp.mean(out * out, axis=-1, keepdims=True)
    y_ref[...] = x_ref[...] + out * lax.rsqrt(ms + EPS) * gp_ref[...]


def _merge(oa, ob, oc, proj, gate_col, x2d, wa_all, wb_all, wc_all, wo_all, gp_all, layer, bm):
    m, d = x2d.shape
    wbr = oa.shape[1]
    assert gate_col % d == 0

    def ospec():
        return pl.BlockSpec((bm, wbr), lambda i: (i, 0))

    def gspec(k):
        return pl.BlockSpec((bm, d), lambda i: (i, gate_col // d + k))

    def wspec(rows):
        return pl.BlockSpec((None, rows, d), lambda i: (layer, 0, 0), pipeline_mode=pl.Buffered(1))

    return pl.pallas_call(
        _merge_kernel,
        grid=(m // bm,),
        in_specs=[ospec(), ospec(), ospec(), gspec(0), gspec(1), gspec(2),
                  pl.BlockSpec((bm, d), lambda i: (i, 0)),
                  wspec(wbr), wspec(wbr), wspec(wbr), wspec(d),
                  pl.BlockSpec((None, 1, d), lambda i: (layer, 0, 0))],
        out_specs=pl.BlockSpec((bm, d), lambda i: (i, 0)),
        out_shape=jax.ShapeDtypeStruct((m, d), F32),
        compiler_params=_cparams(("parallel",)),
        name="merge",
    )(oa, ob, oc, proj, proj, proj, x2d, wa_all, wb_all, wc_all, wo_all, gp_all)


def _kvout_kernel(*refs, depth, heads, dh, tm):
    k_ins, v_ins = refs[:depth], refs[depth:2 * depth]
    ko_ref, vo_ref = refs[2 * depth:]
    layer = pl.program_id(0)
    for l in range(depth):
        @pl.when(layer == l)
        def _(l=l):
            for h in range(heads):
                ko_ref[pl.ds(h, tm, stride=heads), :] = k_ins[l][:, h * dh:(h + 1) * dh]
                vo_ref[pl.ds(h, tm, stride=heads), :] = v_ins[l][:, h * dh:(h + 1) * dh]


def _kvout(projs, heads, dh, ck, cv, tm):
    depth = len(projs)
    m = projs[0].shape[0]
    w = heads * dh
    nt = m // tm

    def inspec(l, col):
        return pl.BlockSpec((tm, w), lambda d_, i: (jnp.where(d_ == l, i, jnp.where(d_ < l, 0, nt - 1)), col // w))

    out_spec = pl.BlockSpec((None, tm * heads, dh), lambda d_, i: (d_, i, 0))
    out_sds = jax.ShapeDtypeStruct((depth, m * heads, dh), F32)
    return pl.pallas_call(
        functools.partial(_kvout_kernel, depth=depth, heads=heads, dh=dh, tm=tm),
        grid=(depth, nt),
        in_specs=[inspec(l, ck) for l in range(depth)] + [inspec(l, cv) for l in range(depth)],
        out_specs=[out_spec, out_spec],
        out_shape=[out_sds, out_sds],
        compiler_params=_cparams(("arbitrary", "arbitrary")),
        name="kvout",
    )(*projs, *projs)


def _pick(n, prefs):
    for p in prefs:
        if n % p == 0:
            return p
    return n


def kernel(x_prompt, x_sample, cache_k, cache_v, state_hgrn, state_lru_h, state_conv, page_table, norm_pre, norm_post, w_in, hgrn_lb_logits, hgrn_norm, lru_conv_w, lru_conv_b, lru_w_r, lru_b_r, lru_w_i, lru_b_i, lru_lambda, sb_bias, w_branch_a, w_branch_b, w_branch_c, w_out):
    depth, d, n_cols = w_in.shape
    pb, seq, _ = x_prompt.shape
    sb, dec_seq, _ = x_sample.shape
    _, _, a_heads, a_dk, a_dv = state_hgrn.shape
    a_kw, a_w = a_heads * a_dk, a_heads * a_dv
    b_w = state_lru_h.shape[2]
    conv_w = lru_conv_w.shape[1]
    _, _, page, c_heads, c_dh = cache_k.shape
    c_w = c_heads * c_dh
    sizes = (a_kw, a_kw, a_w, a_w, b_w, b_w, c_w, c_w, c_w, c_w, d, d, d)
    assert sum(sizes) == n_cols and a_kw == a_w
    offs = np.concatenate([[0], np.cumsum(sizes)]).tolist()
    (o_aq, o_af, o_ai, o_ag, o_bx, o_bg, o_cq, o_ck, o_cv, o_cg, o_ma) = offs[:11]

    lbs = jnp.cumsum(jax.nn.softmax(hgrn_lb_logits.astype(F32), axis=0), axis=0)
    lbs = lbs - lbs[0:1]
    loglb = jnp.log(lbs)[:, None, :]
    log1mlb = jnp.log1p(-lbs)[:, None, :]
    omlb = (1.0 - lbs)[:, None, :]
    r3 = lambda p: p.astype(F32)[:, None, :]
    norm_pre3, norm_post3, hgrn_norm3 = r3(norm_pre), r3(norm_post), r3(hgrn_norm)
    conv_b3, b_r3, b_i3, lam3 = r3(lru_conv_b), r3(lru_b_r), r3(lru_b_i), r3(lru_lambda)
    conv_w3 = lru_conv_w.astype(F32)
    w_in16 = w_in.astype(BF16)
    w_r16, w_i16 = lru_w_r.astype(BF16), lru_w_i.astype(BF16)
    w_a16, w_b16, w_c16, w_o16 = (t.astype(BF16) for t in (w_branch_a, w_branch_b, w_branch_c, w_out))
    sbias = sb_bias.astype(F32)

    lpad = SUBLANES
    assert dec_seq <= lpad and dec_seq >= conv_w - 1
    xs_pad = jnp.pad(x_sample, ((0, 0), (0, lpad - dec_seq), (0, 0)))

    mp, ms = pb * seq, sb * lpad
    bn = _pick(n_cols, (1024, 512, 256, 128))
    hc = _pick(seq, (128, 64, 32, 16, 8))
    tl = _pick(seq, (256, 128, 64, 32, 16, 8))
    tq = _pick(seq, (128,))
    sbg = _pick(seq // tq, (2, 1))
    sbu = sbg
    hgs = _pick(sb, (4, 2, 1))
    hgp = _pick(pb, (2, 1))
    lgs = _pick(sb, (8, 4, 2, 1))
    pps = _pick(page_table.shape[1], (16, 8, 4, 2, 1))
    hp, hs = x_prompt.reshape(mp, d), xs_pad.reshape(ms, d)
    zs_p = jnp.zeros((pb, a_heads, a_dk, a_dv), F32)
    zh_p = jnp.zeros((pb, 1, b_w), F32)
    zb_p = jnp.zeros((pb, conv_w - 1, b_w), F32)
    h0_s = state_lru_h[:, :, None, :]

    outs = [[] for _ in range(4)]
    projs = ([], [])
    s_stack = [None, None]
    for l in range(depth):
        for grp in range(2):
            if grp == 0:
                x2d, bsz, lp, nv = hp, pb, seq, seq
            else:
                x2d, bsz, lp, nv = hs, sb, lpad, dec_seq
            m = x2d.shape[0]
            bm = _pick(m, (1024, 512, 256, 128, 64, 32, 16, 8))
            proj = _inproj(x2d, norm_pre3, w_in16, l, bm, bn)
            main3 = proj.reshape(bsz, lp, n_cols)
            if grp == 0:
                o_a, s_stack[0] = _hgrn(main3, zs_p, s_stack[0], depth, loglb, log1mlb, omlb, hgrn_norm3, l,
                                        hc, hc, hgp, (o_aq, o_af, o_ai, o_ag))
                o_b, h_new, buf_new = _lru(main3, zb_p, zh_p, conv_w3, conv_b3, w_r16, b_r3, w_i16, b_i3, lam3, l,
                                           tl, tl, 1, (o_bx, o_bg))
                o_c = _sb_prompt(main3, sbias, l, c_heads, c_dh, tq, sbg, sbu, (o_cq, o_ck, o_cv, o_cg))
            else:
                o_a, s_stack[1] = _hgrn(main3, state_hgrn, s_stack[1], depth, loglb, log1mlb, omlb, hgrn_norm3, l,
                                        lpad, nv, hgs, (o_aq, o_af, o_ai, o_ag))
                o_b, h_new, buf_new = _lru(main3, state_conv, h0_s, conv_w3, conv_b3, w_r16, b_r3, w_i16, b_i3,
                                           lam3, l, lpad, nv, lgs, (o_bx, o_bg))
                o_c = _sb_sample(main3, cache_k, cache_v, page_table, sbias, l, nv, pps, (o_cq, o_ck, o_cv, o_cg))
            bmm = _pick(m, (256, 128, 64, 32, 16, 8))
            y = _merge(o_a.reshape(m, a_w), o_b.reshape(m, b_w), o_c.reshape(m, c_w), proj, o_ma, x2d,
                       w_a16, w_b16, w_c16, w_o16, norm_post3, l, bmm)
            projs[grp].append(proj)
            for lst, val in zip(outs[grp * 2:(grp + 1) * 2], (h_new[:, 0, :], buf_new)):
                lst.append(val)
            if grp == 0:
                hp = y
            else:
                hs = y
    y_p = hp.reshape(pb, seq, d)
    y_s = hs.reshape(sb, lpad, d)[:, :dec_seq]
    pk, pv = (t.reshape(depth, pb, seq, c_heads, c_dh)
              for t in _kvout(projs[0], c_heads, c_dh, o_ck, o_cv, _pick(mp, (512, 256, 128, 64, 32, 16, 8))))
    sk, sv = (t.reshape(depth, sb, lpad, c_heads, c_dh)[:, :, :dec_seq]
              for t in _kvout(projs[1], c_heads, c_dh, o_ck, o_cv, _pick(ms, (512, 256, 128, 64, 32, 16, 8))))
    ph, pc, sh, sc = (jnp.stack(o) for o in outs)
    return (y_p, y_s, pk, pv, s_stack[0], ph, pc, sk, sv, s_stack[1], sh, sc)
```
